```python
import math
import jax, jax.numpy as jnp
from jax import lax
import numpy as np

D_MODEL = 4096
BATCH = 2
SEQ = 8192
DEPTH = 2

PLE_DIM = 256
D_RWKV = D_MODEL // 2
RWKV_HEAD_DIM = 64
RWKV_HEADS = D_RWKV // RWKV_HEAD_DIM
D_ATTN = D_MODEL - D_RWKV
ATTN_HEAD_DIM = 128
ATTN_HEADS = D_ATTN // ATTN_HEAD_DIM
KV_HEADS = 4
GQA_GROUP = ATTN_HEADS // KV_HEADS
D_KV = KV_HEADS * ATTN_HEAD_DIM
D_MIX = D_RWKV + D_ATTN
WINDOW = 128
BLOCK = 128
HALO = -(-WINDOW // BLOCK)
CONV_WIDTH = 3
DECAY_LORA = max(32, int(round(1.8 * math.sqrt(D_RWKV) / 32)) * 32)
AAA_LORA = max(32, int(round(1.8 * math.sqrt(D_RWKV) / 32)) * 32)
MV_LORA = max(32, int(round(1.3 * math.sqrt(D_RWKV) / 32)) * 32)
COLS_BASE = 4 * D_RWKV + DECAY_LORA + AAA_LORA + 2 * D_ATTN + 2 * D_KV
NORM_EPS = 1e-6
LNX_EPS = 64e-5

kernel_name = "hymba_rwkv7_swa_alibi_ple_encoder"


def rms_norm(x, g):
    xf = x.astype(jnp.float32)
    y = xf * lax.rsqrt(jnp.mean(xf * xf, axis=-1, keepdims=True) + NORM_EPS)
    return (y * g.astype(jnp.float32)).astype(x.dtype)


def split_columns(u, widths):
    offsets = [int(o) for o in np.cumsum(widths)[:-1]]
    return jnp.split(u, offsets, axis=-1)


def centred_conv(u, w):
    half = CONV_WIDTH // 2
    t = u.shape[1]
    up = jnp.pad(u, ((0, 0), (half, half), (0, 0)))
    out = w[0] * up[:, 0:t]
    for j in range(1, CONV_WIDTH):
        out = out + w[j] * up[:, j:j + t]
    return out


def heads_rwkv(u):
    return u.reshape(*u.shape[:-1], RWKV_HEADS, RWKV_HEAD_DIM)


def wkv7_scan(r, decay, k, v, kk, a, reverse):
    xs = tuple(jnp.moveaxis(u, 1, 0) for u in (r, decay, k, v, kk, a))

    def step(state, inp):
        r_t, w_t, k_t, v_t, kk_t, a_t = inp
        sa = jnp.einsum('bhij,bhj->bhi', state, -kk_t)
        state = (state * w_t[:, :, None, :]
                 + sa[..., :, None] * (kk_t * a_t)[:, :, None, :]
                 + v_t[..., :, None] * k_t[:, :, None, :])
        y_t = jnp.einsum('bhij,bhj->bhi', state, r_t)
        return state, y_t

    b, _, h, n = r.shape
    s0 = jnp.zeros((b, h, n, n), jnp.float32)
    _, ys = lax.scan(step, s0, xs, reverse=reverse)
    return jnp.moveaxis(ys, 0, 1)


def rwkv7_bidirectional(rkv, lw, la, conv_w, w0, w2, a0, a2, k_k, k_a, r_k, lnx_g, lnx_b, v_mix):
    f32 = jnp.float32
    b, t, _ = rkv.shape
    r, k, v = jnp.split(centred_conv(rkv, conv_w), 3, axis=-1)
    v_raw = v
    if v_mix is not None:
        v_first, v0, v2, lv = v_mix
        v = v + (v_first - v) * jax.nn.sigmoid(v0 + lv @ v2)
    w_log = -jax.nn.softplus(-(w0[:, None, None, :]
                               + jnp.einsum('btr,drc->dbtc', jnp.tanh(lw), w2)).astype(f32)) - 0.5
    decay = jnp.exp(-jnp.exp(w_log))
    a = jax.nn.sigmoid((a0[:, None, None, :]
                        + jnp.einsum('btr,drc->dbtc', la, a2)).astype(f32))
    rf, kf, vf = heads_rwkv(r.astype(f32)), heads_rwkv(k.astype(f32)), heads_rwkv(v.astype(f32))
    kk = heads_rwkv((k * k_k).astype(f32))
    kk = kk / jnp.maximum(jnp.sqrt(jnp.sum(kk * kk, axis=-1, keepdims=True)), 1e-12)
    k_a_h = heads_rwkv(k_a.astype(f32))
    a_h = heads_rwkv(a)
    decay_h = heads_rwkv(decay)
    k_dir = kf[None] * (1.0 + (a_h - 1.0) * k_a_h)
    y = (wkv7_scan(rf, decay_h[0], k_dir[0], vf, kk, a_h[0], reverse=False)
         + wkv7_scan(rf, decay_h[1], k_dir[1], vf, kk, a_h[1], reverse=True))
    mu = jnp.mean(y, axis=-1, keepdims=True)
    var = jnp.mean(jnp.square(y - mu), axis=-1, keepdims=True)
    yn = ((y - mu) * lax.rsqrt(var + LNX_EPS)).reshape(b, t, D_RWKV)
    k_bonus = kf * (1.0 + (jnp.mean(a_h, axis=0) - 1.0) * k_a_h)
    bonus = jnp.sum(rf * k_bonus * r_k.astype(f32), axis=-1, keepdims=True) * vf
    out = yn * lnx_g.astype(f32) + lnx_b.astype(f32) + bonus.reshape(b, t, D_RWKV)
    return out.astype(rkv.dtype), v_raw


def windowed_gqa_alibi_sink(q, k, v, sink):
    f32 = jnp.float32
    b, t, _ = q.shape
    nb = t // BLOCK
    pad = HALO * BLOCK
    s_w = (2 * HALO + 1) * BLOCK
    qb = q.reshape(b, nb, BLOCK, KV_HEADS, GQA_GROUP, ATTN_HEAD_DIM)

    def band(u):
        up = jnp.pad(u.reshape(b, t, KV_HEADS, ATTN_HEAD_DIM), ((0, 0), (pad, pad), (0, 0), (0, 0)))
        ub = up.reshape(b, nb + 2 * HALO, BLOCK, KV_HEADS, ATTN_HEAD_DIM)
        return jnp.concatenate([ub[:, j:j + nb] for j in range(2 * HALO + 1)], axis=2)

    kw, vw = band(k), band(v)
    scores = jnp.einsum('bnqkgd,bnskd->bnkgqs', qb, kw).astype(f32) / math.sqrt(ATTN_HEAD_DIM)
    qi = jnp.arange(BLOCK)
    sj = jnp.arange(s_w)
    dist = qi[:, None] - sj[None, :] + pad
    s_glob = jnp.arange(nb)[:, None] * BLOCK - pad + sj[None, :]
    valid = (jnp.abs(dist) <= WINDOW)[None, :, :] & ((s_glob >= 0) & (s_glob < t))[:, None, :]
    slopes = jnp.exp2(-8.0 * jnp.arange(1, ATTN_HEADS + 1, dtype=f32) / ATTN_HEADS)
    slopes = slopes.reshape(KV_HEADS, GQA_GROUP)
    logits = scores - slopes[:, :, None, None] * jnp.abs(dist).astype(f32)
    logits = jnp.where(valid[None, :, None, None, :, :], logits, -jnp.inf)
    sink_l = jnp.broadcast_to(sink.astype(f32).reshape(KV_HEADS, GQA_GROUP)[None, None, :, :, None, None],
                              logits.shape[:-1] + (1,))
    probs = jax.nn.softmax(jnp.concatenate([logits, sink_l], axis=-1), axis=-1)[..., :-1]
    out = jnp.einsum('bnkgqs,bnskd->bnqkgd', probs.astype(v.dtype), vw)
    return out.reshape(b, t, D_ATTN)


def setup_inputs(seed: int = 0) -> dict:
    key = jax.random.key(seed)
    ks = jax.random.split(key, 24)

    def nrm(k, shape):
        return jax.random.normal(k, shape, jnp.float32)

    return {
        "x": nrm(ks[0], (BATCH, SEQ, D_MODEL)),
        "p": nrm(ks[1], (DEPTH, BATCH, SEQ, PLE_DIM)),
        "w_in_first": nrm(ks[2], (D_MODEL, COLS_BASE)) * D_MODEL ** -0.5,
        "w_in_rest": nrm(ks[3], (DEPTH - 1, D_MODEL, COLS_BASE + MV_LORA)) * D_MODEL ** -0.5,
        "norm_g": 1.0 + 0.02 * nrm(ks[4], (DEPTH, D_MODEL)),
        "conv_w": jnp.array([0.25, 0.5, 0.25], jnp.float32)[None, :, None]
                  + 0.05 * nrm(ks[5], (DEPTH, CONV_WIDTH, 3 * D_RWKV)),
        "w0": -2.0 + 1.5 * nrm(ks[6], (DEPTH, 2, D_RWKV)),
        "w2": 0.5 * DECAY_LORA ** -0.5 * nrm(ks[7], (DEPTH, 2, DECAY_LORA, D_RWKV)),
        "a0": 0.5 * nrm(ks[8], (DEPTH, 2, D_RWKV)),
        "a2": 0.5 * AAA_LORA ** -0.5 * nrm(ks[9], (DEPTH, 2, AAA_LORA, D_RWKV)),
        "k_k": 0.85 + 0.05 * nrm(ks[10], (DEPTH, D_RWKV)),
        "k_a": 1.0 + 0.05 * nrm(ks[11], (DEPTH, D_RWKV)),
        "r_k": 0.1 * nrm(ks[12], (DEPTH, RWKV_HEADS, RWKV_HEAD_DIM)),
        "lnx_g": 1.0 + 0.02 * nrm(ks[13], (DEPTH, D_RWKV)),
        "lnx_b": 0.01 * nrm(ks[14], (DEPTH, D_RWKV)),
        "v0": 0.5 * nrm(ks[15], (DEPTH - 1, D_RWKV)),
        "v2": 0.5 * MV_LORA ** -0.5 * nrm(ks[16], (DEPTH - 1, MV_LORA, D_RWKV)),
        "attn_sink": 0.5 * nrm(ks[17], (DEPTH, ATTN_HEADS)),
        "w_out": nrm(ks[18], (DEPTH, D_MIX, D_MODEL)) * D_MIX ** -0.5,
        "ple_norm_g": 1.0 + 0.02 * nrm(ks[19], (DEPTH, D_MODEL)),
        "w_ple": nrm(ks[20], (DEPTH, PLE_DIM, D_MODEL)) * PLE_DIM ** -0.5,
        "w_pg": nrm(ks[21], (DEPTH, D_MODEL, D_MODEL)) * D_MODEL ** -0.5,
        "final_g": 1.0 + 0.02 * nrm(ks[22], (D_MODEL,)),
    }


def reference(x, p, w_in_first, w_in_rest, norm_g, conv_w, w0, w2, a0, a2, k_k, k_a, r_k,
              lnx_g, lnx_b, v0, v2, attn_sink, w_out, ple_norm_g, w_ple, w_pg, final_g):
    h = x
    v_first = None
    for i in range(DEPTH):
        xn = rms_norm(h, norm_g[i])
        w_in = w_in_first if i == 0 else w_in_rest[i - 1]
        proj = xn @ w_in
        widths = [3 * D_RWKV, D_RWKV, DECAY_LORA, AAA_LORA, D_ATTN, D_KV, D_KV, D_ATTN]
        if i > 0:
            widths = widths + [MV_LORA]
        parts = split_columns(proj, widths)
        rkv, g_rwkv, lw, la, q, k_att, v_att, g_att = parts[:8]
        v_mix = None if i == 0 else (v_first, v0[i - 1], v2[i - 1], parts[8])
        y_rwkv, v_raw = rwkv7_bidirectional(rkv, lw, la, conv_w[i], w0[i], w2[i], a0[i], a2[i],
                                            k_k[i], k_a[i], r_k[i], lnx_g[i], lnx_b[i], v_mix)
        if i == 0:
            v_first = v_raw
        y_att = windowed_gqa_alibi_sink(q, k_att, v_att, attn_sink[i])
        mixed = jnp.concatenate([y_rwkv * jax.nn.silu(g_rwkv), y_att * jax.nn.silu(g_att)], axis=-1)
        h = h + mixed @ w_out[i]
        gate = jax.nn.sigmoid(rms_norm(h, ple_norm_g[i]) @ w_pg[i])
        h = h + (p[i] @ w_ple[i]) * gate
    return rms_norm(h, final_g)
```

```python
import functools
import math

import jax
import jax.numpy as jnp
from jax import lax
from jax.experimental import pallas as pl
from jax.experimental.pallas import tpu as pltpu

F32 = jnp.float32
BF16 = jnp.bfloat16

D_MODEL = 4096
PLE_DIM = 256
D_RWKV = 2048
HEAD = 64
N_HEADS = D_RWKV // HEAD
D_ATTN = 2048
ATTN_HEAD = 128
KV_HEADS = 4
GQA = 4
D_KV = KV_HEADS * ATTN_HEAD
WINDOW = 128
DECAY_LORA = 96
AAA_LORA = 96
MV_LORA = 64
LORA_PAD = 128
NORM_EPS = 1e-6
LNX_EPS = 64e-5
CHUNK = 64

COL_R, COL_K, COL_V, COL_GR = 0, 2048, 4096, 6144
COL_Q, COL_KA, COL_VA, COL_GA = 8192, 10240, 10752, 11264
N_MAIN = 13312
SRC_LW, SRC_LA, SRC_Q, SRC_LV = 8192, 8288, 8384, 13504

VMEM_LIMIT = 56 * 1024 * 1024


def _sigmoid(x):
    return 1.0 / (1.0 + jnp.exp(-x))


def _params(sem, vmem=VMEM_LIMIT):
    return pltpu.CompilerParams(dimension_semantics=sem, vmem_limit_bytes=vmem)


def _rmsnorm_kernel(x_ref, g_ref, o_ref):
    x = x_ref[...]
    ms = jnp.mean(x * x, axis=-1, keepdims=True)
    o_ref[...] = (x * lax.rsqrt(ms + NORM_EPS) * g_ref[...]).astype(o_ref.dtype)


def _rmsnorm(x, g, out_dtype, tm=256):
    m, d = x.shape
    tm = min(tm, m)
    return pl.pallas_call(
        _rmsnorm_kernel,
        grid=(m // tm,),
        in_specs=[pl.BlockSpec((tm, d), lambda i: (i, 0)),
                  pl.BlockSpec((1, d), lambda i: (0, 0))],
        out_specs=pl.BlockSpec((tm, d), lambda i: (i, 0)),
        out_shape=jax.ShapeDtypeStruct((m, d), out_dtype),
        compiler_params=_params(("parallel",)),
        name="rmsnorm",
    )(x, g.reshape(1, d).astype(F32))


def _mm_kernel(a_ref, w_ref, o_ref):
    o_ref[...] = jnp.dot(a_ref[...], w_ref[...], preferred_element_type=F32).astype(o_ref.dtype)


def _matmul(a, w, out_dtype, tm, tn):
    m, k = a.shape
    n = w.shape[1]
    tm, tn = min(tm, m), min(tn, n)
    return pl.pallas_call(
        _mm_kernel,
        grid=(n // tn, m // tm),
        in_specs=[pl.BlockSpec((tm, k), lambda j, i: (i, 0)),
                  pl.BlockSpec((k, tn), lambda j, i: (0, j))],
        out_specs=pl.BlockSpec((tm, tn), lambda j, i: (i, j)),
        out_shape=jax.ShapeDtypeStruct((m, n), out_dtype),
        compiler_params=_params(("parallel", "parallel")),
        name="in_proj",
    )(a, w)


def _mm_out_kernel(h_ref, a1_ref, a2_ref, w1_ref, w2_ref, o_ref):
    acc = jnp.dot(a1_ref[...], w1_ref[...], preferred_element_type=F32)
    acc = acc + jnp.dot(a2_ref[...], w2_ref[...], preferred_element_type=F32)
    o_ref[...] = h_ref[...] + acc


def _out_proj(h, a1, a2, w1, w2, tm=512, tn=1024):
    m, n = h.shape
    k1, k2 = a1.shape[1], a2.shape[1]
    tm, tn = min(tm, m), min(tn, n)
    return pl.pallas_call(
        _mm_out_kernel,
        grid=(n // tn, m // tm),
        in_specs=[pl.BlockSpec((tm, tn), lambda j, i: (i, j)),
                  pl.BlockSpec((tm, k1), lambda j, i: (i, 0)),
                  pl.BlockSpec((tm, k2), lambda j, i: (i, 0)),
                  pl.BlockSpec((k1, tn), lambda j, i: (0, j)),
                  pl.BlockSpec((k2, tn), lambda j, i: (0, j))],
        out_specs=pl.BlockSpec((tm, tn), lambda j, i: (i, j)),
        out_shape=jax.ShapeDtypeStruct((m, n), F32),
        compiler_params=_params(("parallel", "parallel")),
        name="out_proj",
    )(h, a1, a2, w1, w2)


def _mm_gate_kernel(h_ref, hn_ref, p_ref, wpg_ref, wple_ref, o_ref):
    gate = _sigmoid(jnp.dot(hn_ref[...], wpg_ref[...], preferred_element_type=F32))
    ple = jnp.dot(p_ref[...].astype(BF16), wple_ref[...], preferred_element_type=F32)
    o_ref[...] = h_ref[...] + ple * gate


def _ple_gate(h, hn, p, wpg, wple, tm=512, tn=1024):
    m, n = h.shape
    k = hn.shape[1]
    kp = p.shape[1]
    tm, tn = min(tm, m), min(tn, n)
    return pl.pallas_call(
        _mm_gate_kernel,
        grid=(n // tn, m // tm),
        in_specs=[pl.BlockSpec((tm, tn), lambda j, i: (i, j)),
                  pl.BlockSpec((tm, k), lambda j, i: (i, 0)),
                  pl.BlockSpec((tm, kp), lambda j, i: (i, 0)),
                  pl.BlockSpec((k, tn), lambda j, i: (0, j)),
                  pl.BlockSpec((kp, tn), lambda j, i: (0, j))],
        out_specs=pl.BlockSpec((tm, tn), lambda j, i: (i, j)),
        out_shape=jax.ShapeDtypeStruct((m, n), F32),
        compiler_params=_params(("parallel", "parallel")),
        name="ple_gate",
    )(h, hn, p, wpg, wple)


def _attn_kernel(sink_ref, q_ref, kp_ref, kc_ref, kn_ref, vp_ref, vc_ref, vn_ref, g_ref, o_ref):
    kvh = pl.program_id(1)
    n = pl.program_id(2)
    nb = pl.num_programs(2)
    blk = WINDOW
    q = q_ref[...]
    qs = jnp.concatenate([q[:, g * blk:(g + 1) * blk] for g in range(GQA)], axis=0)
    kw = jnp.concatenate([kp_ref[...], kc_ref[...], kn_ref[...]], axis=0)
    vw = jnp.concatenate([vp_ref[...], vc_ref[...], vn_ref[...]], axis=0)
    s = lax.dot_general(qs, kw, (((1,), (1,)), ((), ())), preferred_element_type=F32)
    s = s / math.sqrt(ATTN_HEAD)
    rows, cols = GQA * blk, 3 * blk
    row = lax.broadcasted_iota(jnp.int32, (rows, cols), 0)
    col = lax.broadcasted_iota(jnp.int32, (rows, cols), 1)
    dist = jnp.abs((row & (blk - 1)) - col + blk)
    rgrp = lax.broadcasted_iota(jnp.int32, (rows, 1), 0) >> 7
    head = (kvh * GQA + rgrp + 1).astype(F32)
    slope = jnp.exp2(-8.0 * head / (KV_HEADS * GQA))
    logits = s - slope * dist.astype(F32)
    valid = (dist <= WINDOW) & ((col >= blk) | (n > 0)) & ((col < 2 * blk) | (n < nb - 1))
    logits = jnp.where(valid, logits, -1e30)
    sink = jnp.zeros((rows, 1), F32)
    for g in range(GQA):
        sink = jnp.where(rgrp == g, sink_ref[kvh * GQA + g], sink)
    m = jnp.maximum(jnp.max(logits, axis=-1, keepdims=True), sink)
    pr = jnp.exp(logits - m)
    den = jnp.sum(pr, axis=-1, keepdims=True) + jnp.exp(sink - m)
    o = jnp.dot(pr.astype(BF16), vw, preferred_element_type=F32) / den
    o = jnp.concatenate([o[g * blk:(g + 1) * blk] for g in range(GQA)], axis=1)
    gate = g_ref[...].astype(F32)
    o_ref[...] = (o * (gate * _sigmoid(gate))).astype(o_ref.dtype)


def _attention(proj, sink):
    b, t, _ = proj.shape
    nb = t // WINDOW
    gw = GQA * ATTN_HEAD

    def kv_spec(col0, off):
        def imap(bi, hi, ni):
            return (bi, jnp.clip(ni + off, 0, nb - 1), col0 // ATTN_HEAD + hi)
        return pl.BlockSpec((None, WINDOW, ATTN_HEAD), imap)

    return pl.pallas_call(
        _attn_kernel,
        grid=(b, KV_HEADS, nb),
        in_specs=[pl.BlockSpec(memory_space=pltpu.SMEM),
                  pl.BlockSpec((None, WINDOW, gw), lambda bi, hi, ni: (bi, ni, COL_Q // gw + hi)),
                  kv_spec(COL_KA, -1), kv_spec(COL_KA, 0), kv_spec(COL_KA, 1),
                  kv_spec(COL_VA, -1), kv_spec(COL_VA, 0), kv_spec(COL_VA, 1),
                  pl.BlockSpec((None, WINDOW, gw), lambda bi, hi, ni: (bi, ni, COL_GA // gw + hi))],
        out_specs=pl.BlockSpec((None, WINDOW, gw), lambda bi, hi, ni: (bi, ni, hi)),
        out_shape=jax.ShapeDtypeStruct((b, t, D_ATTN), BF16),
        compiler_params=_params(("parallel", "parallel", "parallel")),
        name="window_attn",
    )(sink.astype(F32), proj, proj, proj, proj, proj, proj, proj, proj)


def _split3(x):
    hi = x.astype(BF16)
    r1 = x - hi.astype(F32)
    mid = r1.astype(BF16)
    lo = (r1 - mid.astype(F32)).astype(BF16)
    return hi, mid, lo


def _prep_kernel(*refs, tt, cw, vmix):
    (r_m, r_p, r_n, k_m, k_p, k_n, v_m, v_p, v_n, code_ref, cwr_ref, cwk_ref, cwv_ref,
     w0_ref, w2_ref, a0_ref, a2_ref, kk_ref, ka_ref, rk_ref) = refs[:20]
    refs = refs[20:]
    if vmix:
        vf_ref, v0_ref, v2_ref = refs[:3]
        refs = refs[3:]
    q2_ref, kbt_ref, kkt_ref, etb_ref, vout_ref, bonus_ref = refs[:6]
    vraw_ref = None if vmix else refs[6]

    tb = pl.program_id(1)
    ntb = pl.num_programs(1)
    rows = lax.broadcasted_iota(jnp.int32, (tt, cw), 0)

    def conv(m_ref, p_ref, n_ref, w_ref):
        x = m_ref[...].astype(F32)
        prev = jnp.where(tb > 0, p_ref[...].astype(F32)[15:16, :], 0.0)
        nxt = jnp.where(tb < ntb - 1, n_ref[...].astype(F32)[0:1, :], 0.0)
        xm = jnp.where(rows == 0, prev, pltpu.roll(x, 1, 0))
        xp = jnp.where(rows == tt - 1, nxt, pltpu.roll(x, tt - 1, 0))
        w = w_ref[...]
        return w[0:1] * xm + w[1:2] * x + w[2:3] * xp

    xr = conv(r_m, r_p, r_n, cwr_ref)
    xk = conv(k_m, k_p, k_n, cwk_ref)
    xv = conv(v_m, v_p, v_n, cwv_ref)

    code = code_ref[...]
    lw = jnp.tanh(code[:, 0:LORA_PAD]).astype(BF16)
    la = code[:, LORA_PAD:2 * LORA_PAD].astype(BF16)

    if vmix:
        lv = code[:, 2 * LORA_PAD:3 * LORA_PAD].astype(BF16)
        mixg = _sigmoid(v0_ref[...] + jnp.dot(lv, v2_ref[...], preferred_element_type=F32))
        v = xv + (vf_ref[...].astype(F32) - xv) * mixg
    else:
        vraw_ref[...] = xv.astype(vraw_ref.dtype)
        v = xv

    li = lax.broadcasted_iota(jnp.int32, (128, 128), 0) >> 6
    lj = lax.broadcasted_iota(jnp.int32, (128, 128), 1) >> 6
    gmat = (li == lj).astype(BF16)

    def head_sum(x):
        parts = [jnp.dot(x[:, g * 128:(g + 1) * 128].astype(BF16), gmat, preferred_element_type=F32)
                 for g in range(cw // 128)]
        return jnp.concatenate(parts, axis=1)

    kkn = xk * kk_ref[...]
    kappa = kkn / jnp.maximum(jnp.sqrt(head_sum(kkn * kkn)), 1e-12)
    ka = ka_ref[...]

    ti = lax.broadcasted_iota(jnp.int32, (tt, tt), 0)
    tj = lax.broadcasted_iota(jnp.int32, (tt, tt), 1)
    same = (ti >> 6) == (tj >> 6)
    tri = (jnp.where(same & (tj <= ti), 1.0, 0.0).astype(BF16),
           jnp.where(same & (tj >= ti), 1.0, 0.0).astype(BF16))
    ones = jnp.where(same, 1.0, 0.0).astype(BF16)

    a_sum = None
    for d in range(2):
        wl = w0_ref[d:d + 1, :] + jnp.dot(lw, w2_ref[d], preferred_element_type=F32)
        z = -wl
        softplus = jnp.maximum(z, 0.0) + jnp.log(1.0 + jnp.exp(-jnp.abs(z)))
        ld = -jnp.exp(-softplus - 0.5)
        a = _sigmoid(a0_ref[d:d + 1, :] + jnp.dot(la, a2_ref[d], preferred_element_type=F32))
        a_sum = a if a_sum is None else a_sum + a
        kd = xk * (1.0 + (a - 1.0) * ka)
        bvec = kappa * a
        parts = _split3(ld)
        sel = jnp.concatenate([tri[d], ones], axis=0)
        cs2 = sum(jnp.dot(sel, pt, preferred_element_type=F32) for pt in parts)
        cs, tot = cs2[:tt], cs2[tt:]
        e_pos = jnp.exp(cs)
        e_neg = jnp.exp(-cs)
        qk = kappa * jnp.exp(cs - ld)
        qr = xr * e_pos
        kb = bvec * e_neg
        kk = kd * e_neg
        et = jnp.exp(tot)
        for g in range(cw // 128):
            sl = slice(g * 128, (g + 1) * 128)
            kbt = kb[:, sl].T
            kkt = kk[:, sl].T
            ett = et[:, sl].T
            for hh in range(2):
                hl = 2 * g + hh
                hs = slice(hh * HEAD, (hh + 1) * HEAD)
                ls = slice(hl * HEAD, (hl + 1) * HEAD)
                for cc in range(tt // CHUNK):
                    cs_ = slice(cc * CHUNK, (cc + 1) * CHUNK)
                    q2_ref[d, hl, cc, 0:CHUNK, :] = qk[cs_, ls].astype(BF16)
                    q2_ref[d, hl, cc, CHUNK:2 * CHUNK, :] = qr[cs_, ls].astype(BF16)
                    kbt_ref[d, hl, cc] = kbt[hs, cs_].astype(BF16)
                    kkt_ref[d, hl, cc] = kkt[hs, cs_].astype(BF16)
                    etb_ref[d, hl, cc] = ett[hs, cs_]

    for hl in range(cw // HEAD):
        ls = slice(hl * HEAD, (hl + 1) * HEAD)
        for cc in range(tt // CHUNK):
            vout_ref[hl, cc] = v[cc * CHUNK:(cc + 1) * CHUNK, ls].astype(BF16)

    k_bonus = xk * (1.0 + (a_sum * 0.5 - 1.0) * ka)
    bonus = head_sum(xr * k_bonus * rk_ref[...]) * v
    bonus_ref[...] = bonus.astype(bonus_ref.dtype)


def _rwkv_prep(proj, codes, conv_w, w0, w2, a0, a2, k_k, k_a, r_k, vmix, tt=256, cw=256):
    b, t, _ = proj.shape
    tt = min(tt, t)
    nc = t // CHUNK
    ntb = t // tt
    hb = 16

    def main_spec(col0):
        return pl.BlockSpec((None, tt, cw), lambda bi, ti, ci: (bi, ti, col0 // cw + ci))

    def prev_spec(col0):
        return pl.BlockSpec((None, hb, cw),
                            lambda bi, ti, ci: (bi, jnp.maximum(ti * (tt // hb) - 1, 0), col0 // cw + ci))

    def next_spec(col0):
        return pl.BlockSpec((None, hb, cw),
                            lambda bi, ti, ci: (bi, jnp.minimum((ti + 1) * (tt // hb), t // hb - 1), col0 // cw + ci))

    def chan_spec(rows_, col0=0):
        return pl.BlockSpec((rows_, cw), lambda bi, ti, ci: (0, col0 // cw + ci))

    lora_spec = pl.BlockSpec((2, LORA_PAD, cw), lambda bi, ti, ci: (0, 0, ci))
    in_specs = []
    args = []
    for col0 in (COL_R, COL_K, COL_V):
        in_specs += [main_spec(col0), prev_spec(col0), next_spec(col0)]
        args += [proj, proj, proj]
    in_specs += [pl.BlockSpec((None, tt, 3 * LORA_PAD), lambda bi, ti, ci: (bi, ti, 0)),
                 chan_spec(3, COL_R), chan_spec(3, COL_K), chan_spec(3, COL_V),
                 chan_spec(2), lora_spec, chan_spec(2), lora_spec,
                 chan_spec(1), chan_spec(1), chan_spec(1)]
    args += [codes, conv_w, conv_w, conv_w, w0, w2, a0, a2, k_k, k_a, r_k]
    if vmix is not None:
        v_first, v0, v2 = vmix
        in_specs += [pl.BlockSpec((None, tt, cw), lambda bi, ti, ci: (bi, ti, ci)),
                     chan_spec(1),
                     pl.BlockSpec((LORA_PAD, cw), lambda bi, ti, ci: (0, ci))]
        args += [v_first, v0, v2]

    nh_b, nc_b = cw // HEAD, tt // CHUNK

    def hm_spec(rows_, lead2):
        blk = ((2,) if lead2 else ()) + (None, nh_b, nc_b, rows_, HEAD)
        if lead2:
            return pl.BlockSpec(blk, lambda bi, ti, ci: (0, bi, ci, ti, 0, 0))
        return pl.BlockSpec(blk, lambda bi, ti, ci: (bi, ci, ti, 0, 0))

    tok_spec = pl.BlockSpec((None, tt, cw), lambda bi, ti, ci: (bi, ti, ci))
    out_specs = [hm_spec(2 * CHUNK, True), hm_spec(HEAD, True), hm_spec(HEAD, True), hm_spec(HEAD, True),
                 hm_spec(CHUNK, False), tok_spec]
    out_shape = [jax.ShapeDtypeStruct((2, b, N_HEADS, nc, 2 * CHUNK, HEAD), BF16),
                 jax.ShapeDtypeStruct((2, b, N_HEADS, nc, HEAD, CHUNK), BF16),
                 jax.ShapeDtypeStruct((2, b, N_HEADS, nc, HEAD, CHUNK), BF16),
                 jax.ShapeDtypeStruct((2, b, N_HEADS, nc, HEAD, CHUNK), F32),
                 jax.ShapeDtypeStruct((b, N_HEADS, nc, CHUNK, HEAD), BF16),
                 jax.ShapeDtypeStruct((b, t, D_RWKV), BF16)]
    if vmix is None:
        out_specs.append(tok_spec)
        out_shape.append(jax.ShapeDtypeStruct((b, t, D_RWKV), BF16))

    return pl.pallas_call(
        functools.partial(_prep_kernel, tt=tt, cw=cw, vmix=vmix is not None),
        grid=(b, ntb, D_RWKV // cw),
        in_specs=in_specs,
        out_specs=out_specs,
        out_shape=out_shape,
        compiler_params=_params(("parallel", "parallel", "parallel")),
        name="rwkv_prep",
    )(*args)


def _scan_kernel(q2f, kbtf, kktf, etf, vf, q2b, kbtb, kktb, etb, vb, yf_ref, yb_ref, h_ref, *, hb):
    c = pl.program_id(2)

    @pl.when(c == 0)
    def _():
        h_ref[...] = jnp.zeros_like(h_ref)

    ti = lax.broadcasted_iota(jnp.int32, (CHUNK, CHUNK), 0)
    si = lax.broadcasted_iota(jnp.int32, (CHUNK, CHUNK), 1)
    eye = (ti == si).astype(F32)

    def dot(a, b_):
        return jnp.dot(a.astype(BF16), b_.astype(BF16), preferred_element_type=F32)

    dirs = ((q2f, kbtf, kktf, etf, vf, yf_ref, si < ti, si <= ti),
            (q2b, kbtb, kktb, etb, vb, yb_ref, si > ti, si >= ti))
    for d, (q2_r, kbt_r, kkt_r, et_r, v_r, y_r, strict, incl) in enumerate(dirs):
        for j in range(hb):
            q2 = q2_r[j]
            kbt = kbt_r[j]
            kkt = kkt_r[j]
            v = v_r[j]
            h0 = h_ref[d, j]
            ab = dot(q2, kbt)
            ak = dot(q2, kkt)
            qh = dot(q2, h0)
            x = jnp.where(strict, -ab[:CHUNK], 0.0)
            a_ak = jnp.where(strict, ak[:CHUNK], 0.0)
            a_rb = jnp.where(incl, ab[CHUNK:], 0.0)
            a_rk = jnp.where(incl, ak[CHUNK:], 0.0)
            tinv = eye + x
            pw = x
            for _ in range(int(math.log2(CHUNK)) - 1):
                pw = dot(pw, pw)
                tinv = tinv + dot(tinv, pw)
            u = dot(tinv, qh[:CHUNK] + dot(a_ak, v))
            y_r[j] = qh[CHUNK:] + dot(a_rk, v) - dot(a_rb, u)
            h_ref[d, j] = et_r[j] * (h0 + dot(kkt, v) - dot(kbt, u))


def _rwkv_scan(q2, kbt, kkt, etb, v, hb=4):
    _, b, nh, nc, _, _ = q2.shape

    def spec(rows_, d):
        def imap(bi, gi, ci):
            return (d, bi, gi, ci if d == 0 else nc - 1 - ci, 0, 0)
        return pl.BlockSpec((None, None, hb, None, rows_, HEAD), imap)

    def vspec(d):
        def imap(bi, gi, ci):
            return (bi, gi, ci if d == 0 else nc - 1 - ci, 0, 0)
        return pl.BlockSpec((None, hb, None, CHUNK, HEAD), imap)

    in_specs, args = [], []
    for d in range(2):
        in_specs += [spec(2 * CHUNK, d), spec(HEAD, d), spec(HEAD, d), spec(HEAD, d), vspec(d)]
        args += [q2, kbt, kkt, etb, v]
    yshape = jax.ShapeDtypeStruct((b, nh, nc, CHUNK, HEAD), F32)
    return pl.pallas_call(
        functools.partial(_scan_kernel, hb=hb),
        grid=(b, nh // hb, nc),
        in_specs=in_specs,
        out_specs=[vspec(0), vspec(1)],
        out_shape=[yshape, yshape],
        scratch_shapes=[pltpu.VMEM((2, hb, HEAD, HEAD), F32)],
        compiler_params=_params(("parallel", "parallel", "arbitrary")),
        name="rwkv_scan",
    )(*args)


def _post_kernel(yf_ref, yb_ref, bonus_ref, g_ref, lg_ref, lb_ref, o_ref, ybuf):
    for h in range(N_HEADS):
        ybuf[:, h * HEAD:(h + 1) * HEAD] = yf_ref[h] + yb_ref[h]
    li = lax.broadcasted_iota(jnp.int32, (128, 128), 0) >> 6
    lj = lax.broadcasted_iota(jnp.int32, (128, 128), 1) >> 6
    gmat = (li == lj).astype(BF16)
    for g in range(D_RWKV // 128):
        sl = slice(g * 128, (g + 1) * 128)
        y = ybuf[:, sl]
        mu = jnp.dot(y.astype(BF16), gmat, preferred_element_type=F32) * (1.0 / HEAD)
        dy = y - mu
        var = jnp.dot((dy * dy).astype(BF16), gmat, preferred_element_type=F32) * (1.0 / HEAD)
        yn = dy * lax.rsqrt(var + LNX_EPS)
        out = yn * lg_ref[:, sl] + lb_ref[:, sl] + bonus_ref[:, sl].astype(F32)
        gate = g_ref[:, sl].astype(F32)
        o_ref[:, sl] = (out * (gate * _sigmoid(gate))).astype(o_ref.dtype)


def _rwkv_post(yf, yb, bonus, proj, lnx_g, lnx_b, tt=128):
    b, t, _ = bonus.shape
    tt = min(tt, t)
    yspec = pl.BlockSpec((None, N_HEADS, tt, HEAD), lambda bi, ti: (bi, 0, ti, 0))
    return pl.pallas_call(
        _post_kernel,
        grid=(b, t // tt),
        in_specs=[yspec, yspec,
                  pl.BlockSpec((None, tt, D_RWKV), lambda bi, ti: (bi, ti, 0)),
                  pl.BlockSpec((None, tt, D_RWKV), lambda bi, ti: (bi, ti, COL_GR // D_RWKV)),
                  pl.BlockSpec((1, D_RWKV), lambda bi, ti: (0, 0)),
                  pl.BlockSpec((1, D_RWKV), lambda bi, ti: (0, 0))],
        out_specs=pl.BlockSpec((None, tt, D_RWKV), lambda bi, ti: (bi, ti, 0)),
        out_shape=jax.ShapeDtypeStruct((b, t, D_RWKV), BF16),
        scratch_shapes=[pltpu.VMEM((tt, D_RWKV), F32)],
        compiler_params=_params(("parallel", "parallel")),
        name="rwkv_post",
    )(yf.reshape(b, N_HEADS, t, HEAD), yb.reshape(b, N_HEADS, t, HEAD), bonus, proj, lnx_g, lnx_b)


def _pad_rows(w, rows):
    pad = [(0, 0)] * w.ndim
    pad[-2] = (0, rows - w.shape[-2])
    return jnp.pad(w, pad)


def kernel(x, p, w_in_first, w_in_rest, norm_g, conv_w, w0, w2, a0, a2, k_k, k_a, r_k, lnx_g, lnx_b,
           v0, v2, attn_sink, w_out, ple_norm_g, w_ple, w_pg, final_g):
    b, t, d = x.shape
    m = b * t
    depth = norm_g.shape[0]
    h = x.reshape(m, d)
    v_first = None
    for i in range(depth):
        w_in = w_in_first if i == 0 else w_in_rest[i - 1]
        w_main = jnp.concatenate([w_in[:, :SRC_LW], w_in[:, SRC_Q:SRC_LV]], axis=1).astype(BF16)
        lora_cols = [w_in[:, SRC_LW:SRC_LA], w_in[:, SRC_LA:SRC_Q]]
        if i > 0:
            lora_cols.append(w_in[:, SRC_LV:SRC_LV + MV_LORA])
        else:
            lora_cols.append(jnp.zeros((d, MV_LORA), w_in.dtype))
        w_lora = jnp.concatenate(
            [jnp.pad(c, ((0, 0), (0, LORA_PAD - c.shape[1]))) for c in lora_cols], axis=1).astype(BF16)

        xn = _rmsnorm(h, norm_g[i], BF16)
        proj = _matmul(xn, w_main, BF16, tm=512, tn=1024).reshape(b, t, N_MAIN)
        codes = _matmul(xn, w_lora, F32, tm=512, tn=3 * LORA_PAD).reshape(b, t, 3 * LORA_PAD)

        vmix = None
        if i > 0:
            vmix = (v_first, v0[i - 1].reshape(1, D_RWKV), _pad_rows(v2[i - 1], LORA_PAD).astype(BF16))
        outs = _rwkv_prep(proj, codes, conv_w[i], w0[i], _pad_rows(w2[i], LORA_PAD).astype(BF16),
                          a0[i], _pad_rows(a2[i], LORA_PAD).astype(BF16),
                          k_k[i].reshape(1, D_RWKV), k_a[i].reshape(1, D_RWKV), r_k[i].reshape(1, D_RWKV), vmix)
        q2, kbt, kkt, etb, vh, bonus = outs[:6]
        if i == 0:
            v_first = outs[6]
        yf, yb = _rwkv_scan(q2, kbt, kkt, etb, vh)
        mixed_r = _rwkv_post(yf, yb, bonus, proj, lnx_g[i].reshape(1, D_RWKV), lnx_b[i].reshape(1, D_RWKV))
        mixed_a = _attention(proj, attn_sink[i])

        wo = w_out[i].astype(BF16)
        h = _out_proj(h, mixed_r.reshape(m, D_RWKV), mixed_a.reshape(m, D_ATTN), wo[:D_RWKV], wo[D_RWKV:])
        hn = _rmsnorm(h, ple_norm_g[i], BF16)
        h = _ple_gate(h, hn, p[i].reshape(m, PLE_DIM), w_pg[i].astype(BF16), w_ple[i].astype(BF16))
    return _rmsnorm(h, final_g, F32).reshape(b, t, d)
```

```python
import functools
import math

import jax
import jax.numpy as jnp
from jax import lax
from jax.experimental import pallas as pl
from jax.experimental.pallas import tpu as pltpu

F32 = jnp.float32
BF16 = jnp.bfloat16

D_MODEL = 4096
PLE_DIM = 256
D_RWKV = 2048
HEAD = 64
N_HEADS = D_RWKV // HEAD
D_ATTN = 2048
ATTN_HEAD = 128
KV_HEADS = 4
GQA = 4
D_KV = KV_HEADS * ATTN_HEAD
WINDOW = 128
DECAY_LORA = 96
AAA_LORA = 96
MV_LORA = 64
LORA_PAD = 128
NORM_EPS = 1e-6
LNX_EPS = 64e-5
CHUNK = 64

COL_R, COL_K, COL_V, COL_GR = 0, 2048, 4096, 6144
COL_Q, COL_KA, COL_VA, COL_GA = 8192, 10240, 10752, 11264
N_MAIN = 13312
SRC_LW, SRC_LA, SRC_Q, SRC_LV = 8192, 8288, 8384, 13504

VMEM_LIMIT = 56 * 1024 * 1024


def _sigmoid(x):
    return 1.0 / (1.0 + jnp.exp(-x))


def _params(sem, vmem=VMEM_LIMIT):
    return pltpu.CompilerParams(dimension_semantics=sem, vmem_limit_bytes=vmem)


def _rmsnorm_kernel(x_ref, g_ref, o_ref):
    x = x_ref[...]
    ms = jnp.mean(x * x, axis=-1, keepdims=True)
    o_ref[...] = (x * lax.rsqrt(ms + NORM_EPS) * g_ref[...]).astype(o_ref.dtype)


def _rmsnorm(x, g, out_dtype, tm=256):
    m, d = x.shape
    tm = min(tm, m)
    return pl.pallas_call(
        _rmsnorm_kernel,
        grid=(m // tm,),
        in_specs=[pl.BlockSpec((tm, d), lambda i: (i, 0)),
                  pl.BlockSpec((1, d), lambda i: (0, 0))],
        out_specs=pl.BlockSpec((tm, d), lambda i: (i, 0)),
        out_shape=jax.ShapeDtypeStruct((m, d), out_dtype),
        compiler_params=_params(("parallel",)),
        name="rmsnorm",
    )(x, g.reshape(1, d).astype(F32))


def _mm_kernel(a_ref, w_ref, o_ref):
    o_ref[...] = jnp.dot(a_ref[...], w_ref[...], preferred_element_type=F32).astype(o_ref.dtype)


def _matmul(a, w, out_dtype, tm, tn):
    m, k = a.shape
    n = w.shape[1]
    tm, tn = min(tm, m), min(tn, n)
    return pl.pallas_call(
        _mm_kernel,
        grid=(n // tn, m // tm),
        in_specs=[pl.BlockSpec((tm, k), lambda j, i: (i, 0)),
                  pl.BlockSpec((k, tn), lambda j, i: (0, j))],
        out_specs=pl.BlockSpec((tm, tn), lambda j, i: (i, j)),
        out_shape=jax.ShapeDtypeStruct((m, n), out_dtype),
        compiler_params=_params(("parallel", "parallel")),
        name="in_proj",
    )(a, w)


def _mm_out_kernel(h_ref, a1_ref, a2_ref, w1_ref, w2_ref, o_ref):
    acc = jnp.dot(a1_ref[...], w1_ref[...], preferred_element_type=F32)
    acc = acc + jnp.dot(a2_ref[...], w2_ref[...], preferred_element_type=F32)
    o_ref[...] = h_ref[...] + acc


def _out_proj(h, a1, a2, w1, w2, tm=512, tn=1024):
    m, n = h.shape
    k1, k2 = a1.shape[1], a2.shape[1]
    tm, tn = min(tm, m), min(tn, n)
    return pl.pallas_call(
        _mm_out_kernel,
        grid=(n // tn, m // tm),
        in_specs=[pl.BlockSpec((tm, tn), lambda j, i: (i, j)),
                  pl.BlockSpec((tm, k1), lambda j, i: (i, 0)),
                  pl.BlockSpec((tm, k2), lambda j, i: (i, 0)),
                  pl.BlockSpec((k1, tn), lambda j, i: (0, j)),
                  pl.BlockSpec((k2, tn), lambda j, i: (0, j))],
        out_specs=pl.BlockSpec((tm, tn), lambda j, i: (i, j)),
        out_shape=jax.ShapeDtypeStruct((m, n), F32),
        compiler_params=_params(("parallel", "parallel")),
        name="out_proj",
    )(h, a1, a2, w1, w2)


def _mm_gate_kernel(h_ref, hn_ref, p_ref, wpg_ref, wple_ref, o_ref):
    gate = _sigmoid(jnp.dot(hn_ref[...], wpg_ref[...], preferred_element_type=F32))
    ple = jnp.dot(p_ref[...].astype(BF16), wple_ref[...], preferred_element_type=F32)
    o_ref[...] = h_ref[...] + ple * gate


def _ple_gate(h, hn, p, wpg, wple, tm=512, tn=1024):
    m, n = h.shape
    k = hn.shape[1]
    kp = p.shape[1]
    tm, tn = min(tm, m), min(tn, n)
    return pl.pallas_call(
        _mm_gate_kernel,
        grid=(n // tn, m // tm),
        in_specs=[pl.BlockSpec((tm, tn), lambda j, i: (i, j)),
                  pl.BlockSpec((tm, k), lambda j, i: (i, 0)),
                  pl.BlockSpec((tm, kp), lambda j, i: (i, 0)),
                  pl.BlockSpec((k, tn), lambda j, i: (0, j)),
                  pl.BlockSpec((kp, tn), lambda j, i: (0, j))],
        out_specs=pl.BlockSpec((tm, tn), lambda j, i: (i, j)),
        out_shape=jax.ShapeDtypeStruct((m, n), F32),
        compiler_params=_params(("parallel", "parallel")),
        name="ple_gate",
    )(h, hn, p, wpg, wple)


def _attn_kernel(sink_ref, q_ref, kp_ref, kc_ref, kn_ref, vp_ref, vc_ref, vn_ref, g_ref, o_ref):
    kvh = pl.program_id(1)
    n = pl.program_id(2)
    nb = pl.num_programs(2)
    blk = WINDOW
    q = q_ref[...]
    qs = jnp.concatenate([q[:, g * blk:(g + 1) * blk] for g in range(GQA)], axis=0)
    kw = jnp.concatenate([kp_ref[...], kc_ref[...], kn_ref[...]], axis=0)
    vw = jnp.concatenate([vp_ref[...], vc_ref[...], vn_ref[...]], axis=0)
    s = lax.dot_general(qs, kw, (((1,), (1,)), ((), ())), preferred_element_type=F32)
    s = s / math.sqrt(ATTN_HEAD)
    rows, cols = GQA * blk, 3 * blk
    row = lax.broadcasted_iota(jnp.int32, (rows, cols), 0)
    col = lax.broadcasted_iota(jnp.int32, (rows, cols), 1)
    dist = jnp.abs((row & (blk - 1)) - col + blk)
    rgrp = lax.broadcasted_iota(jnp.int32, (rows, 1), 0) >> 7
    head = (kvh * GQA + rgrp + 1).astype(F32)
    slope = jnp.exp2(-8.0 * head / (KV_HEADS * GQA))
    logits = s - slope * dist.astype(F32)
    valid = (dist <= WINDOW) & ((col >= blk) | (n > 0)) & ((col < 2 * blk) | (n < nb - 1))
    logits = jnp.where(valid, logits, -1e30)
    sink = jnp.zeros((rows, 1), F32)
    for g in range(GQA):
        sink = jnp.where(rgrp == g, sink_ref[kvh * GQA + g], sink)
    m = jnp.maximum(jnp.max(logits, axis=-1, keepdims=True), sink)
    pr = jnp.exp(logits - m)
    den = jnp.sum(pr, axis=-1, keepdims=True) + jnp.exp(sink - m)
    o = jnp.dot(pr.astype(BF16), vw, preferred_element_type=F32) / den
    o = jnp.concatenate([o[g * blk:(g + 1) * blk] for g in range(GQA)], axis=1)
    gate = g_ref[...].astype(F32)
    o_ref[...] = (o * (gate * _sigmoid(gate))).astype(o_ref.dtype)


def _attention(proj, sink):
    b, t, _ = proj.shape
    nb = t // WINDOW
    gw = GQA * ATTN_HEAD

    def kv_spec(col0, off):
        def imap(bi, hi, ni):
            return (bi, jnp.clip(ni + off, 0, nb - 1), col0 // ATTN_HEAD + hi)
        return pl.BlockSpec((None, WINDOW, ATTN_HEAD), imap)

    return pl.pallas_call(
        _attn_kernel,
        grid=(b, KV_HEADS, nb),
        in_specs=[pl.BlockSpec(memory_space=pltpu.SMEM),
                  pl.BlockSpec((None, WINDOW, gw), lambda bi, hi, ni: (bi, ni, COL_Q // gw + hi)),
                  kv_spec(COL_KA, -1), kv_spec(COL_KA, 0), kv_spec(COL_KA, 1),
                  kv_spec(COL_VA, -1), kv_spec(COL_VA, 0), kv_spec(COL_VA, 1),
                  pl.BlockSpec((None, WINDOW, gw), lambda bi, hi, ni: (bi, ni, COL_GA // gw + hi))],
        out_specs=pl.BlockSpec((None, WINDOW, gw), lambda bi, hi, ni: (bi, ni, hi)),
        out_shape=jax.ShapeDtypeStruct((b, t, D_ATTN), BF16),
        compiler_params=_params(("parallel", "parallel", "parallel")),
        name="window_attn",
    )(sink.astype(F32), proj, proj, proj, proj, proj, proj, proj, proj)


def _split3(x):
    hi = x.astype(BF16)
    r1 = x - hi.astype(F32)
    mid = r1.astype(BF16)
    lo = (r1 - mid.astype(F32)).astype(BF16)
    return hi, mid, lo


def _prep_kernel(*refs, tt, cw, vmix):
    (r_m, r_p, r_n, k_m, k_p, k_n, v_m, v_p, v_n, code_ref, cwr_ref, cwk_ref, cwv_ref,
     w0_ref, w2_ref, a0_ref, a2_ref, kk_ref, ka_ref, rk_ref) = refs[:20]
    refs = refs[20:]
    if vmix:
        vf_ref, v0_ref, v2_ref = refs[:3]
        refs = refs[3:]
    q2_ref, kt2_ref, etb_ref, vout_ref, bonus_ref = refs[:5]
    vraw_ref = None if vmix else refs[5]

    tb = pl.program_id(1)
    ntb = pl.num_programs(1)
    rows = lax.broadcasted_iota(jnp.int32, (tt, cw), 0)

    def conv(m_ref, p_ref, n_ref, w_ref):
        x = m_ref[...].astype(F32)
        prev = jnp.where(tb > 0, p_ref[...].astype(F32)[15:16, :], 0.0)
        nxt = jnp.where(tb < ntb - 1, n_ref[...].astype(F32)[0:1, :], 0.0)
        xm = jnp.where(rows == 0, prev, pltpu.roll(x, 1, 0))
        xp = jnp.where(rows == tt - 1, nxt, pltpu.roll(x, tt - 1, 0))
        w = w_ref[...]
        return w[0:1] * xm + w[1:2] * x + w[2:3] * xp

    xr = conv(r_m, r_p, r_n, cwr_ref)
    xk = conv(k_m, k_p, k_n, cwk_ref)
    xv = conv(v_m, v_p, v_n, cwv_ref)

    code = code_ref[...]
    lw = jnp.tanh(code[:, 0:LORA_PAD]).astype(BF16)
    la = code[:, LORA_PAD:2 * LORA_PAD].astype(BF16)

    if vmix:
        lv = code[:, 2 * LORA_PAD:3 * LORA_PAD].astype(BF16)
        mixg = _sigmoid(v0_ref[...] + jnp.dot(lv, v2_ref[...], preferred_element_type=F32))
        v = xv + (vf_ref[...].astype(F32) - xv) * mixg
    else:
        vraw_ref[...] = xv.astype(vraw_ref.dtype)
        v = xv

    li = lax.broadcasted_iota(jnp.int32, (128, 128), 0) >> 6
    lj = lax.broadcasted_iota(jnp.int32, (128, 128), 1) >> 6
    gmat = (li == lj).astype(BF16)

    def head_sum(x):
        parts = [jnp.dot(x[:, g * 128:(g + 1) * 128].astype(BF16), gmat, preferred_element_type=F32)
                 for g in range(cw // 128)]
        return jnp.concatenate(parts, axis=1)

    kkn = xk * kk_ref[...]
    kappa = kkn / jnp.maximum(jnp.sqrt(head_sum(kkn * kkn)), 1e-12)
    ka = ka_ref[...]

    ti = lax.broadcasted_iota(jnp.int32, (tt, tt), 0)
    tj = lax.broadcasted_iota(jnp.int32, (tt, tt), 1)
    same = (ti >> 6) == (tj >> 6)
    tri = (jnp.where(same & (tj <= ti), 1.0, 0.0).astype(BF16),
           jnp.where(same & (tj >= ti), 1.0, 0.0).astype(BF16))
    ones = jnp.where(same, 1.0, 0.0).astype(BF16)

    a_sum = None
    for d in range(2):
        wl = w0_ref[d:d + 1, :] + jnp.dot(lw, w2_ref[d], preferred_element_type=F32)
        z = -wl
        softplus = jnp.maximum(z, 0.0) + jnp.log(1.0 + jnp.exp(-jnp.abs(z)))
        ld = -jnp.exp(-softplus - 0.5)
        a = _sigmoid(a0_ref[d:d + 1, :] + jnp.dot(la, a2_ref[d], preferred_element_type=F32))
        a_sum = a if a_sum is None else a_sum + a
        kd = xk * (1.0 + (a - 1.0) * ka)
        bvec = kappa * a
        parts = _split3(ld)
        sel = jnp.concatenate([tri[d], ones], axis=0)
        cs2 = sum(jnp.dot(sel, pt, preferred_element_type=F32) for pt in parts)
        cs, tot = cs2[:tt], cs2[tt:]
        e_pos = jnp.exp(cs)
        e_neg = jnp.exp(-cs)
        qk = kappa * jnp.exp(cs - ld)
        qr = xr * e_pos
        kb = bvec * e_neg
        kk = kd * e_neg
        et = jnp.exp(tot)
        for g in range(cw // 128):
            sl = slice(g * 128, (g + 1) * 128)
            kbt = kb[:, sl].T
            kkt = kk[:, sl].T
            ett = et[:, sl].T
            for hh in range(2):
                hl = 2 * g + hh
                hs = slice(hh * HEAD, (hh + 1) * HEAD)
                ls = slice(hl * HEAD, (hl + 1) * HEAD)
                for cc in range(tt // CHUNK):
                    cs_ = slice(cc * CHUNK, (cc + 1) * CHUNK)
                    q2_ref[d, hl, cc, 0:CHUNK, :] = qk[cs_, ls].astype(BF16)
                    q2_ref[d, hl, cc, CHUNK:2 * CHUNK, :] = qr[cs_, ls].astype(BF16)
                    kt2_ref[d, hl, cc, :, 0:CHUNK] = kbt[hs, cs_].astype(BF16)
                    kt2_ref[d, hl, cc, :, CHUNK:2 * CHUNK] = kkt[hs, cs_].astype(BF16)
                    etb_ref[d, hl, cc] = ett[hs, cs_]

    for hl in range(cw // HEAD):
        ls = slice(hl * HEAD, (hl + 1) * HEAD)
        for cc in range(tt // CHUNK):
            vout_ref[hl, cc] = v[cc * CHUNK:(cc + 1) * CHUNK, ls].astype(BF16)

    k_bonus = xk * (1.0 + (a_sum * 0.5 - 1.0) * ka)
    bonus = head_sum(xr * k_bonus * rk_ref[...]) * v
    bonus_ref[...] = bonus.astype(bonus_ref.dtype)


def _rwkv_prep(proj, codes, conv_w, w0, w2, a0, a2, k_k, k_a, r_k, vmix, tt=256, cw=256):
    b, t, _ = proj.shape
    tt = min(tt, t)
    nc = t // CHUNK
    ntb = t // tt
    hb = 16

    def main_spec(col0):
        return pl.BlockSpec((None, tt, cw), lambda bi, ti, ci: (bi, ti, col0 // cw + ci))

    def prev_spec(col0):
        return pl.BlockSpec((None, hb, cw),
                            lambda bi, ti, ci: (bi, jnp.maximum(ti * (tt // hb) - 1, 0), col0 // cw + ci))

    def next_spec(col0):
        return pl.BlockSpec((None, hb, cw),
                            lambda bi, ti, ci: (bi, jnp.minimum((ti + 1) * (tt // hb), t // hb - 1), col0 // cw + ci))

    def chan_spec(rows_, col0=0):
        return pl.BlockSpec((rows_, cw), lambda bi, ti, ci: (0, col0 // cw + ci))

    lora_spec = pl.BlockSpec((2, LORA_PAD, cw), lambda bi, ti, ci: (0, 0, ci))
    in_specs = []
    args = []
    for col0 in (COL_R, COL_K, COL_V):
        in_specs += [main_spec(col0), prev_spec(col0), next_spec(col0)]
        args += [proj, proj, proj]
    in_specs += [pl.BlockSpec((None, tt, 3 * LORA_PAD), lambda bi, ti, ci: (bi, ti, 0)),
                 chan_spec(3, COL_R), chan_spec(3, COL_K), chan_spec(3, COL_V),
                 chan_spec(2), lora_spec, chan_spec(2), lora_spec,
                 chan_spec(1), chan_spec(1), chan_spec(1)]
    args += [codes, conv_w, conv_w, conv_w, w0, w2, a0, a2, k_k, k_a, r_k]
    if vmix is not None:
        v_first, v0, v2 = vmix
        in_specs += [pl.BlockSpec((None, tt, cw), lambda bi, ti, ci: (bi, ti, ci)),
                     chan_spec(1),
                     pl.BlockSpec((LORA_PAD, cw), lambda bi, ti, ci: (0, ci))]
        args += [v_first, v0, v2]

    nh_b, nc_b = cw // HEAD, tt // CHUNK

    def hm_spec(rows_, cols_, lead2):
        blk = ((2,) if lead2 else ()) + (None, nh_b, nc_b, rows_, cols_)
        if lead2:
            return pl.BlockSpec(blk, lambda bi, ti, ci: (0, bi, ci, ti, 0, 0))
        return pl.BlockSpec(blk, lambda bi, ti, ci: (bi, ci, ti, 0, 0))

    tok_spec = pl.BlockSpec((None, tt, cw), lambda bi, ti, ci: (bi, ti, ci))
    out_specs = [hm_spec(2 * CHUNK, HEAD, True), hm_spec(HEAD, 2 * CHUNK, True), hm_spec(HEAD, CHUNK, True),
                 hm_spec(CHUNK, HEAD, False), tok_spec]
    out_shape = [jax.ShapeDtypeStruct((2, b, N_HEADS, nc, 2 * CHUNK, HEAD), BF16),
                 jax.ShapeDtypeStruct((2, b, N_HEADS, nc, HEAD, 2 * CHUNK), BF16),
                 jax.ShapeDtypeStruct((2, b, N_HEADS, nc, HEAD, CHUNK), F32),
                 jax.ShapeDtypeStruct((b, N_HEADS, nc, CHUNK, HEAD), BF16),
                 jax.ShapeDtypeStruct((b, t, D_RWKV), BF16)]
    if vmix is None:
        out_specs.append(tok_spec)
        out_shape.append(jax.ShapeDtypeStruct((b, t, D_RWKV), BF16))

    return pl.pallas_call(
        functools.partial(_prep_kernel, tt=tt, cw=cw, vmix=vmix is not None),
        grid=(b, ntb, D_RWKV // cw),
        in_specs=in_specs,
        out_specs=out_specs,
        out_shape=out_shape,
        compiler_params=_params(("parallel", "parallel", "parallel")),
        name="rwkv_prep",
    )(*args)


def _scan_kernel(q2f, ktf, etf, vf, q2b, ktb, etb, vb, yf_ref, yb_ref, h_ref, *, hb):
    c = pl.program_id(2)

    @pl.when(c == 0)
    def _():
        h_ref[...] = jnp.zeros_like(h_ref)

    ti = lax.broadcasted_iota(jnp.int32, (CHUNK, 2 * CHUNK), 0)
    si = lax.broadcasted_iota(jnp.int32, (CHUNK, 2 * CHUNK), 1) & (CHUNK - 1)
    strict = (si < ti, si > ti)
    incl = (si <= ti, si >= ti)
    ei = lax.broadcasted_iota(jnp.int32, (CHUNK, CHUNK), 0)
    ej = lax.broadcasted_iota(jnp.int32, (CHUNK, CHUNK), 1)
    eye = (ei == ej).astype(F32)
    zeros_v = jnp.zeros((CHUNK, HEAD), BF16)

    def dot(a, b_):
        return jnp.dot(a.astype(BF16), b_.astype(BF16), preferred_element_type=F32)

    refs = ((q2f, ktf, etf, vf, yf_ref), (q2b, ktb, etb, vb, yb_ref))
    units = [(d, j) for d in range(2) for j in range(hb)]
    q2 = [refs[d][0][j] for d, j in units]
    kt = [refs[d][1][j] for d, j in units]
    v = [refs[d][3][j] for d, j in units]
    h0 = [h_ref[d, j] for d, j in units]
    n = range(len(units))

    a = [dot(q2[u], kt[u]) for u in n]
    qh = [dot(q2[u], h0[u]) for u in n]
    a1 = [jnp.where(strict[units[u][0]], a[u][:CHUNK], 0.0) for u in n]
    a2 = [jnp.where(incl[units[u][0]], a[u][CHUNK:], 0.0) for u in n]
    zv = [jnp.concatenate([zeros_v, v[u]], axis=0) for u in n]
    rhs = [qh[u][:CHUNK] + dot(a1[u], zv[u]) for u in n]
    x = [-a1[u][:, :CHUNK] for u in n]
    t = [eye + x[u] for u in n]
    p = [dot(x[u], x[u]) for u in n]
    for _ in range(int(math.log2(CHUNK)) - 2):
        tp = [dot(jnp.concatenate([t[u], p[u]], axis=0), p[u]) for u in n]
        t = [t[u] + tp[u][:CHUNK] for u in n]
        p = [tp[u][CHUNK:] for u in n]
    t = [t[u] + dot(t[u], p[u]) for u in n]
    uu = [dot(t[u], rhs[u]) for u in n]
    uv = [jnp.concatenate([(-uu[u]).astype(BF16), v[u]], axis=0) for u in n]
    for u in n:
        d, j = units[u]
        refs[d][4][j] = qh[u][CHUNK:] + dot(a2[u], uv[u])
    for u in n:
        d, j = units[u]
        h_ref[d, j] = refs[d][2][j] * (h0[u] + dot(kt[u], uv[u]))


def _rwkv_scan(q2, kt2, etb, v, hb=8):
    _, b, nh, nc, _, _ = q2.shape

    def spec(rows_, cols_, d):
        def imap(bi, gi, ci):
            return (d, bi, gi, ci if d == 0 else nc - 1 - ci, 0, 0)
        return pl.BlockSpec((None, None, hb, None, rows_, cols_), imap)

    def vspec(d):
        def imap(bi, gi, ci):
            return (bi, gi, ci if d == 0 else nc - 1 - ci, 0, 0)
        return pl.BlockSpec((None, hb, None, CHUNK, HEAD), imap)

    in_specs, args = [], []
    for d in range(2):
        in_specs += [spec(2 * CHUNK, HEAD, d), spec(HEAD, 2 * CHUNK, d), spec(HEAD, CHUNK, d), vspec(d)]
        args += [q2, kt2, etb, v]
    yshape = jax.ShapeDtypeStruct((b, nh, nc, CHUNK, HEAD), F32)
    return pl.pallas_call(
        functools.partial(_scan_kernel, hb=hb),
        grid=(b, nh // hb, nc),
        in_specs=in_specs,
        out_specs=[vspec(0), vspec(1)],
        out_shape=[yshape, yshape],
        scratch_shapes=[pltpu.VMEM((2, hb, HEAD, HEAD), F32)],
        compiler_params=_params(("parallel", "parallel", "arbitrary")),
        name="rwkv_scan",
    )(*args)


def _post_kernel(yf_ref, yb_ref, bonus_ref, g_ref, lg_ref, lb_ref, o_ref, ybuf):
    for h in range(N_HEADS):
        ybuf[:, h * HEAD:(h + 1) * HEAD] = yf_ref[h] + yb_ref[h]
    li = lax.broadcasted_iota(jnp.int32, (128, 128), 0) >> 6
    lj = lax.broadcasted_iota(jnp.int32, (128, 128), 1) >> 6
    gmat = (li == lj).astype(BF16)
    for g in range(D_RWKV // 128):
        sl = slice(g * 128, (g + 1) * 128)
        y = ybuf[:, sl]
        mu = jnp.dot(y.astype(BF16), gmat, preferred_element_type=F32) * (1.0 / HEAD)
        dy = y - mu
        var = jnp.dot((dy * dy).astype(BF16), gmat, preferred_element_type=F32) * (1.0 / HEAD)
        yn = dy * lax.rsqrt(var + LNX_EPS)
        out = yn * lg_ref[:, sl] + lb_ref[:, sl] + bonus_ref[:, sl].astype(F32)
        gate = g_ref[:, sl].astype(F32)
        o_ref[:, sl] = (out * (gate * _sigmoid(gate))).astype(o_ref.dtype)


def _rwkv_post(yf, yb, bonus, proj, lnx_g, lnx_b, tt=128):
    b, t, _ = bonus.shape
    tt = min(tt, t)
    yspec = pl.BlockSpec((None, N_HEADS, tt, HEAD), lambda bi, ti: (bi, 0, ti, 0))
    return pl.pallas_call(
        _post_kernel,
        grid=(b, t // tt),
        in_specs=[yspec, yspec,
                  pl.BlockSpec((None, tt, D_RWKV), lambda bi, ti: (bi, ti, 0)),
                  pl.BlockSpec((None, tt, D_RWKV), lambda bi, ti: (bi, ti, COL_GR // D_RWKV)),
                  pl.BlockSpec((1, D_RWKV), lambda bi, ti: (0, 0)),
                  pl.BlockSpec((1, D_RWKV), lambda bi, ti: (0, 0))],
        out_specs=pl.BlockSpec((None, tt, D_RWKV), lambda bi, ti: (bi, ti, 0)),
        out_shape=jax.ShapeDtypeStruct((b, t, D_RWKV), BF16),
        scratch_shapes=[pltpu.VMEM((tt, D_RWKV), F32)],
        compiler_params=_params(("parallel", "parallel")),
        name="rwkv_post",
    )(yf.reshape(b, N_HEADS, t, HEAD), yb.reshape(b, N_HEADS, t, HEAD), bonus, proj, lnx_g, lnx_b)


def _pad_rows(w, rows):
    pad = [(0, 0)] * w.ndim
    pad[-2] = (0, rows - w.shape[-2])
    return jnp.pad(w, pad)


def kernel(x, p, w_in_first, w_in_rest, norm_g, conv_w, w0, w2, a0, a2, k_k, k_a, r_k, lnx_g, lnx_b,
           v0, v2, attn_sink, w_out, ple_norm_g, w_ple, w_pg, final_g):
    b, t, d = x.shape
    m = b * t
    depth = norm_g.shape[0]
    h = x.reshape(m, d)
    v_first = None
    for i in range(depth):
        w_in = w_in_first if i == 0 else w_in_rest[i - 1]
        w_main = jnp.concatenate([w_in[:, :SRC_LW], w_in[:, SRC_Q:SRC_LV]], axis=1).astype(BF16)
        lora_cols = [w_in[:, SRC_LW:SRC_LA], w_in[:, SRC_LA:SRC_Q]]
        if i > 0:
            lora_cols.append(w_in[:, SRC_LV:SRC_LV + MV_LORA])
        else:
            lora_cols.append(jnp.zeros((d, MV_LORA), w_in.dtype))
        w_lora = jnp.concatenate(
            [jnp.pad(c, ((0, 0), (0, LORA_PAD - c.shape[1]))) for c in lora_cols], axis=1).astype(BF16)

        xn = _rmsnorm(h, norm_g[i], BF16)
        proj = _matmul(xn, w_main, BF16, tm=512, tn=1024).reshape(b, t, N_MAIN)
        codes = _matmul(xn, w_lora, F32, tm=512, tn=3 * LORA_PAD).reshape(b, t, 3 * LORA_PAD)

        vmix = None
        if i > 0:
            vmix = (v_first, v0[i - 1].reshape(1, D_RWKV), _pad_rows(v2[i - 1], LORA_PAD).astype(BF16))
        outs = _rwkv_prep(proj, codes, conv_w[i], w0[i], _pad_rows(w2[i], LORA_PAD).astype(BF16),
                          a0[i], _pad_rows(a2[i], LORA_PAD).astype(BF16),
                          k_k[i].reshape(1, D_RWKV), k_a[i].reshape(1, D_RWKV), r_k[i].reshape(1, D_RWKV), vmix)
        q2, kt2, etb, vh, bonus = outs[:5]
        if i == 0:
            v_first = outs[5]
        yf, yb = _rwkv_scan(q2, kt2, etb, vh)
        mixed_r = _rwkv_post(yf, yb, bonus, proj, lnx_g[i].reshape(1, D_RWKV), lnx_b[i].reshape(1, D_RWKV))
        mixed_a = _attention(proj, attn_sink[i])

        wo = w_out[i].astype(BF16)
        h = _out_proj(h, mixed_r.reshape(m, D_RWKV), mixed_a.reshape(m, D_ATTN), wo[:D_RWKV], wo[D_RWKV:])
        hn = _rmsnorm(h, ple_norm_g[i], BF16)
        h = _ple_gate(h, hn, p[i].reshape(m, PLE_DIM), w_pg[i].astype(BF16), w_ple[i].astype(BF16))
    return _rmsnorm(h, final_g, F32).reshape(b, t, d)
```

```python
import functools
import math

import jax
import jax.numpy as jnp
from jax import lax
from jax.experimental import pallas as pl
from jax.experimental.pallas import tpu as pltpu

F32 = jnp.float32
BF16 = jnp.bfloat16

D_MODEL = 4096
PLE_DIM = 256
D_RWKV = 2048
HEAD = 64
LANES = 128
D_ATTN = 2048
ATTN_HEAD = 128
KV_HEADS = 4
GQA = 4
WINDOW = 128
MV_LORA = 64
LORA_PAD = 128
NORM_EPS = 1e-6
LNX_EPS = 64e-5
LOG2E = 1.4426950408889634
CHUNK = 64

COL_R, COL_K, COL_V, COL_GR = 0, 2048, 4096, 6144
COL_Q, COL_KA, COL_VA, COL_GA = 8192, 10240, 10752, 11264
N_MAIN = 13312
SRC_LW, SRC_LA, SRC_Q, SRC_LV = 8192, 8288, 8384, 13504

VMEM_LIMIT = 56 * 1024 * 1024


def _sigmoid(x):
    return 1.0 / (1.0 + jnp.exp(-x))


def _params(sem, vmem=VMEM_LIMIT):
    return pltpu.CompilerParams(dimension_semantics=sem, vmem_limit_bytes=vmem)


def _norm_rows(x, g):
    ms = jnp.mean(x * x, axis=-1, keepdims=True)
    return x * lax.rsqrt(ms + NORM_EPS) * g


def _rmsnorm_kernel(x_ref, g_ref, o_ref):
    o_ref[...] = _norm_rows(x_ref[...], g_ref[...]).astype(o_ref.dtype)


def _rmsnorm(x, g, out_dtype, tm=256):
    m, d = x.shape
    tm = min(tm, m)
    return pl.pallas_call(
        _rmsnorm_kernel,
        grid=(m // tm,),
        in_specs=[pl.BlockSpec((tm, d), lambda i: (i, 0)),
                  pl.BlockSpec((1, d), lambda i: (0, 0))],
        out_specs=pl.BlockSpec((tm, d), lambda i: (i, 0)),
        out_shape=jax.ShapeDtypeStruct((m, d), out_dtype),
        compiler_params=_params(("parallel",)),
        name="rmsnorm",
    )(x, g.reshape(1, d).astype(F32))


def _mm_kernel(a_ref, w_ref, o_ref):
    o_ref[...] = jnp.dot(a_ref[...], w_ref[...], preferred_element_type=F32).astype(o_ref.dtype)


def _matmul(a, w, out_dtype, tm, tn):
    m, k = a.shape
    n = w.shape[1]
    tm, tn = min(tm, m), min(tn, n)
    return pl.pallas_call(
        _mm_kernel,
        grid=(n // tn, m // tm),
        in_specs=[pl.BlockSpec((tm, k), lambda j, i: (i, 0)),
                  pl.BlockSpec((k, tn), lambda j, i: (0, j))],
        out_specs=pl.BlockSpec((tm, tn), lambda j, i: (i, j)),
        out_shape=jax.ShapeDtypeStruct((m, n), out_dtype),
        compiler_params=_params(("parallel", "parallel")),
        name="in_proj",
    )(a, w)


def _norm_mm_kernel(h_ref, g_ref, w_ref, o_ref, xn_ref):
    @pl.when(pl.program_id(1) == 0)
    def _():
        xn_ref[...] = _norm_rows(h_ref[...], g_ref[...]).astype(xn_ref.dtype)

    o_ref[...] = jnp.dot(xn_ref[...], w_ref[...], preferred_element_type=F32).astype(o_ref.dtype)


def _norm_matmul(h, g, w, tm=512, tn=1024):
    m, k = h.shape
    n = w.shape[1]
    tm, tn = min(tm, m), min(tn, n)
    return pl.pallas_call(
        _norm_mm_kernel,
        grid=(m // tm, n // tn),
        in_specs=[pl.BlockSpec((tm, k), lambda i, j: (i, 0)),
                  pl.BlockSpec((1, k), lambda i, j: (0, 0)),
                  pl.BlockSpec((k, tn), lambda i, j: (0, j))],
        out_specs=[pl.BlockSpec((tm, tn), lambda i, j: (i, j)),
                   pl.BlockSpec((tm, k), lambda i, j: (i, 0))],
        out_shape=[jax.ShapeDtypeStruct((m, n), BF16), jax.ShapeDtypeStruct((m, k), BF16)],
        compiler_params=_params(("parallel", "arbitrary")),
        name="norm_in_proj",
    )(h, g.reshape(1, k).astype(F32), w)


def _mm_out_kernel(h_ref, a1_ref, a2_ref, w1_ref, w2_ref, o_ref):
    acc = jnp.dot(a1_ref[...], w1_ref[...], preferred_element_type=F32)
    acc = acc + jnp.dot(a2_ref[...], w2_ref[...], preferred_element_type=F32)
    o_ref[...] = h_ref[...] + acc


def _out_proj(h, a1, a2, w, tm=512, tn=1024):
    m, n = h.shape
    k1, k2 = a1.shape[1], a2.shape[1]
    assert k1 == k2
    tm, tn = min(tm, m), min(tn, n)
    return pl.pallas_call(
        _mm_out_kernel,
        grid=(n // tn, m // tm),
        in_specs=[pl.BlockSpec((tm, tn), lambda j, i: (i, j)),
                  pl.BlockSpec((tm, k1), lambda j, i: (i, 0)),
                  pl.BlockSpec((tm, k2), lambda j, i: (i, 0)),
                  pl.BlockSpec((k1, tn), lambda j, i: (0, j)),
                  pl.BlockSpec((k2, tn), lambda j, i: (1, j))],
        out_specs=pl.BlockSpec((tm, tn), lambda j, i: (i, j)),
        out_shape=jax.ShapeDtypeStruct((m, n), F32),
        compiler_params=_params(("parallel", "parallel")),
        name="out_proj",
    )(h, a1, a2, w, w)


def _norm_gate_kernel(h_ref, g_ref, p_ref, wpg_ref, wple_ref, o_ref, hn_ref, *, tn):
    j = pl.program_id(1)

    @pl.when(j == 0)
    def _():
        hn_ref[...] = _norm_rows(h_ref[...], g_ref[...]).astype(hn_ref.dtype)

    gate = _sigmoid(jnp.dot(hn_ref[...], wpg_ref[...], preferred_element_type=F32))
    ple = jnp.dot(p_ref[...].astype(BF16), wple_ref[...], preferred_element_type=F32)
    o_ref[...] = h_ref[:, pl.ds(pl.multiple_of(j * tn, tn), tn)] + ple * gate


def _ple_gate(h, g, p, wpg, wple, tm=512, tn=1024):
    m, n = h.shape
    kp = p.shape[1]
    tm, tn = min(tm, m), min(tn, n)
    return pl.pallas_call(
        functools.partial(_norm_gate_kernel, tn=tn),
        grid=(m // tm, n // tn),
        in_specs=[pl.BlockSpec((tm, n), lambda i, j: (i, 0)),
                  pl.BlockSpec((1, n), lambda i, j: (0, 0)),
                  pl.BlockSpec((tm, kp), lambda i, j: (i, 0)),
                  pl.BlockSpec((n, tn), lambda i, j: (0, j)),
                  pl.BlockSpec((kp, tn), lambda i, j: (0, j))],
        out_specs=pl.BlockSpec((tm, tn), lambda i, j: (i, j)),
        out_shape=jax.ShapeDtypeStruct((m, n), F32),
        scratch_shapes=[pltpu.VMEM((tm, n), BF16)],
        compiler_params=_params(("parallel", "arbitrary")),
        name="ple_gate",
    )(h, g.reshape(1, n).astype(F32), p, wpg, wple)


def _attn_kernel(sink_ref, q_ref, kp_ref, kc_ref, kn_ref, vp_ref, vc_ref, vn_ref, g_ref, o_ref, bias_ref):
    kvh = pl.program_id(1)
    n = pl.program_id(2)
    nb = pl.num_programs(2)
    blk = WINDOW
    rows, cols = GQA * blk, 3 * blk

    @pl.when(n == 0)
    def _():
        row = lax.broadcasted_iota(jnp.int32, (rows, cols), 0)
        col = lax.broadcasted_iota(jnp.int32, (rows, cols), 1)
        dist = jnp.abs((row & (blk - 1)) - col + blk)
        rgrp = lax.broadcasted_iota(jnp.int32, (rows, 1), 0) >> 7
        head = (kvh * GQA + rgrp + 1).astype(F32)
        slope = jnp.exp2(-8.0 * head / (KV_HEADS * GQA))
        bias_ref[...] = jnp.where(dist <= WINDOW, -slope * dist.astype(F32), -1e30)

    q = q_ref[...]
    qs = jnp.concatenate([q[:, g * blk:(g + 1) * blk] for g in range(GQA)], axis=0)
    kw = jnp.concatenate([kp_ref[...], kc_ref[...], kn_ref[...]], axis=0)
    vw = jnp.concatenate([vp_ref[...], vc_ref[...], vn_ref[...]], axis=0)
    s = lax.dot_general(qs, kw, (((1,), (1,)), ((), ())), preferred_element_type=F32)
    colr = lax.broadcasted_iota(jnp.int32, (1, cols), 1)
    lo = jnp.where(n > 0, 0, blk)
    hi = jnp.where(n < nb - 1, cols, 2 * blk)
    inside = (colr >= lo) & (colr < hi)
    probs, dens = [], []
    for g in range(GQA):
        rs = slice(g * blk, (g + 1) * blk)
        logits = s[rs] * (1.0 / math.sqrt(ATTN_HEAD)) + bias_ref[rs, :]
        logits = jnp.where(inside, logits, -1e30)
        sink = sink_ref[kvh * GQA + g]
        m = jnp.maximum(jnp.max(logits, axis=-1, keepdims=True), sink)
        pr = jnp.exp(logits - m)
        dens.append(jnp.sum(pr, axis=-1, keepdims=True) + jnp.exp(sink - m))
        probs.append(pr.astype(BF16))
    o = jnp.dot(jnp.concatenate(probs, axis=0), vw, preferred_element_type=F32)
    o = jnp.concatenate([o[g * blk:(g + 1) * blk] / dens[g] for g in range(GQA)], axis=1)
    gate = g_ref[...].astype(F32)
    o_ref[...] = (o * (gate * _sigmoid(gate))).astype(o_ref.dtype)


def _attention(proj, sink):
    b, t, _ = proj.shape
    nb = t // WINDOW
    gw = GQA * ATTN_HEAD

    def kv_spec(col0, off):
        def imap(bi, hi, ni):
            return (bi, jnp.clip(ni + off, 0, nb - 1), col0 // ATTN_HEAD + hi)
        return pl.BlockSpec((None, WINDOW, ATTN_HEAD), imap)

    return pl.pallas_call(
        _attn_kernel,
        grid=(b, KV_HEADS, nb),
        in_specs=[pl.BlockSpec(memory_space=pltpu.SMEM),
                  pl.BlockSpec((None, WINDOW, gw), lambda bi, hi, ni: (bi, ni, COL_Q // gw + hi)),
                  kv_spec(COL_KA, -1), kv_spec(COL_KA, 0), kv_spec(COL_KA, 1),
                  kv_spec(COL_VA, -1), kv_spec(COL_VA, 0), kv_spec(COL_VA, 1),
                  pl.BlockSpec((None, WINDOW, gw), lambda bi, hi, ni: (bi, ni, COL_GA // gw + hi))],
        out_specs=pl.BlockSpec((None, WINDOW, gw), lambda bi, hi, ni: (bi, ni, hi)),
        out_shape=jax.ShapeDtypeStruct((b, t, D_ATTN), BF16),
        scratch_shapes=[pltpu.VMEM((GQA * WINDOW, 3 * WINDOW), F32)],
        compiler_params=_params(("parallel", "parallel", "arbitrary")),
        name="window_attn",
    )(sink.astype(F32), proj, proj, proj, proj, proj, proj, proj, proj)


def _prep_kernel(*refs, tt, cw, vmix):
    (r_m, r_p, r_n, k_m, k_p, k_n, v_m, v_p, v_n, code_ref, cwr_ref, cwk_ref, cwv_ref,
     w0_ref, w2_ref, a0_ref, a2_ref, kk_ref, ka_ref, rk_ref) = refs[:20]
    refs = refs[20:]
    if vmix:
        vf_ref, v0_ref, v2_ref = refs[:3]
        refs = refs[3:]
    qk_ref, qr_ref, kb_ref, kd_ref, et_ref, vout_ref, bonus_ref = refs

    tb = pl.program_id(1)
    ntb = pl.num_programs(1)
    rows = lax.broadcasted_iota(jnp.int32, (tt, cw), 0)

    def conv(m_ref, p_ref, n_ref, w_ref):
        x = m_ref[...].astype(F32)
        prev = jnp.where(tb > 0, p_ref[...].astype(F32)[15:16, :], 0.0)
        nxt = jnp.where(tb < ntb - 1, n_ref[...].astype(F32)[0:1, :], 0.0)
        xm = jnp.where(rows == 0, prev, pltpu.roll(x, 1, 0))
        xp = jnp.where(rows == tt - 1, nxt, pltpu.roll(x, tt - 1, 0))
        w = w_ref[...]
        return w[0:1] * xm + w[1:2] * x + w[2:3] * xp

    xr = conv(r_m, r_p, r_n, cwr_ref)
    xk = conv(k_m, k_p, k_n, cwk_ref)
    xv = conv(v_m, v_p, v_n, cwv_ref)

    code = code_ref[...]
    lw = jnp.tanh(code[:, 0:LORA_PAD]).astype(BF16)
    la = code[:, LORA_PAD:2 * LORA_PAD].astype(BF16)

    if vmix:
        lv = code[:, 2 * LORA_PAD:3 * LORA_PAD].astype(BF16)
        mixg = _sigmoid(v0_ref[...] + jnp.dot(lv, v2_ref[...], preferred_element_type=F32))
        v = xv + (vf_ref[...].astype(F32) - xv) * mixg
    else:
        v = xv
    vout_ref[...] = v.astype(vout_ref.dtype)

    li = lax.broadcasted_iota(jnp.int32, (LANES, LANES), 0) >> 6
    lj = lax.broadcasted_iota(jnp.int32, (LANES, LANES), 1) >> 6
    gmat = (li == lj).astype(BF16)

    def head_sum(x):
        parts = [jnp.dot(x[:, g * LANES:(g + 1) * LANES].astype(BF16), gmat, preferred_element_type=F32)
                 for g in range(cw // LANES)]
        return jnp.concatenate(parts, axis=1)

    kkn = xk * kk_ref[...]
    kappa = kkn / jnp.maximum(jnp.sqrt(head_sum(kkn * kkn)), 1e-12)
    ka = ka_ref[...]

    ti = lax.broadcasted_iota(jnp.int32, (tt, tt), 0)
    tj = lax.broadcasted_iota(jnp.int32, (tt, tt), 1)
    same = (ti >> 6) == (tj >> 6)
    tri = (jnp.where(same & (tj <= ti), 1.0, 0.0).astype(BF16),
           jnp.where(same & (tj >= ti), 1.0, 0.0).astype(BF16))
    ones = jnp.where(same, 1.0, 0.0).astype(BF16)

    a_sum = None
    for d in range(2):
        wl = w0_ref[d:d + 1, :] + jnp.dot(lw, w2_ref[d], preferred_element_type=F32)
        z = -wl
        softplus = jnp.maximum(z, 0.0) + jnp.log(1.0 + jnp.exp(-jnp.abs(z)))
        ld = jnp.exp(-softplus - 0.5) * (-LOG2E)
        a = _sigmoid(a0_ref[d:d + 1, :] + jnp.dot(la, a2_ref[d], preferred_element_type=F32))
        a_sum = a if a_sum is None else a_sum + a
        ld_hi = ld.astype(BF16)
        ld_lo = (ld - ld_hi.astype(F32)).astype(BF16)
        sel = jnp.concatenate([tri[d], ones], axis=0)
        cs2 = (jnp.dot(sel, ld_hi, preferred_element_type=F32)
               + jnp.dot(sel, ld_lo, preferred_element_type=F32))
        cs, tot = cs2[:tt], cs2[tt:]
        e_neg = jnp.exp2(-cs)
        qk_ref[d] = (kappa * jnp.exp2(cs - ld)).astype(BF16)
        qr_ref[d] = (xr * jnp.exp2(cs)).astype(BF16)
        kb_ref[d] = (kappa * a * e_neg).astype(BF16)
        kd_ref[d] = (xk * (1.0 + (a - 1.0) * ka) * e_neg).astype(BF16)
        for cc in range(tt // CHUNK):
            et_ref[d, cc] = jnp.exp2(tot[cc * CHUNK:cc * CHUNK + 1, :])

    k_bonus = xk * (1.0 + (a_sum * 0.5 - 1.0) * ka)
    bonus = head_sum(xr * k_bonus * rk_ref[...]) * v
    bonus_ref[...] = bonus.astype(bonus_ref.dtype)


def _rwkv_prep(proj, codes, conv_w, w0, w2, a0, a2, k_k, k_a, r_k, vmix, tt=256, cw=512):
    b, t, _ = proj.shape
    tt = min(tt, t)
    nc = t // CHUNK
    ntb = t // tt
    hb = 16

    def main_spec(col0):
        return pl.BlockSpec((None, tt, cw), lambda bi, ti, ci: (bi, ti, col0 // cw + ci))

    def prev_spec(col0):
        return pl.BlockSpec((None, hb, cw),
                            lambda bi, ti, ci: (bi, jnp.maximum(ti * (tt // hb) - 1, 0), col0 // cw + ci))

    def next_spec(col0):
        return pl.BlockSpec((None, hb, cw),
                            lambda bi, ti, ci: (bi, jnp.minimum((ti + 1) * (tt // hb), t // hb - 1), col0 // cw + ci))

    def chan_spec(rows_, col0=0):
        return pl.BlockSpec((rows_, cw), lambda bi, ti, ci: (0, col0 // cw + ci))

    lora_spec = pl.BlockSpec((2, LORA_PAD, cw), lambda bi, ti, ci: (0, 0, ci))
    in_specs = []
    args = []
    for col0 in (COL_R, COL_K, COL_V):
        in_specs += [main_spec(col0), prev_spec(col0), next_spec(col0)]
        args += [proj, proj, proj]
    in_specs += [pl.BlockSpec((None, tt, 3 * LORA_PAD), lambda bi, ti, ci: (bi, ti, 0)),
                 chan_spec(3, COL_R), chan_spec(3, COL_K), chan_spec(3, COL_V),
                 chan_spec(2), lora_spec, chan_spec(2), lora_spec,
                 chan_spec(1), chan_spec(1), chan_spec(1)]
    args += [codes, conv_w, conv_w, conv_w, w0, w2, a0, a2, k_k, k_a, r_k]
    if vmix is not None:
        v_first, v0, v2 = vmix
        in_specs += [pl.BlockSpec((None, tt, cw), lambda bi, ti, ci: (bi, ti, ci)),
                     chan_spec(1),
                     pl.BlockSpec((LORA_PAD, cw), lambda bi, ti, ci: (0, ci))]
        args += [v_first, v0, v2]

    dir_spec = pl.BlockSpec((2, None, tt, cw), lambda bi, ti, ci: (0, bi, ti, ci))
    tok_spec = pl.BlockSpec((None, tt, cw), lambda bi, ti, ci: (bi, ti, ci))
    et_spec = pl.BlockSpec((2, None, tt // CHUNK, 1, cw), lambda bi, ti, ci: (0, bi, ti, 0, ci))
    dir_shape = jax.ShapeDtypeStruct((2, b, t, D_RWKV), BF16)
    tok_shape = jax.ShapeDtypeStruct((b, t, D_RWKV), BF16)
    out_specs = [dir_spec, dir_spec, dir_spec, dir_spec, et_spec, tok_spec, tok_spec]
    out_shape = [dir_shape, dir_shape, dir_shape, dir_shape,
                 jax.ShapeDtypeStruct((2, b, nc, 1, D_RWKV), F32), tok_shape, tok_shape]
    return pl.pallas_call(
        functools.partial(_prep_kernel, tt=tt, cw=cw, vmix=vmix is not None),
        grid=(b, ntb, D_RWKV // cw),
        in_specs=in_specs,
        out_specs=out_specs,
        out_shape=out_shape,
        compiler_params=_params(("parallel", "parallel", "parallel")),
        name="rwkv_prep",
    )(*args)


def _scan_kernel(qkf, qrf, kbf, kdf, etf, vf, qkb, qrb, kbb, kdb, etb, vb, yf_ref, yb_ref, g_ref, *, pb):
    c = pl.program_id(2)

    @pl.when(c == 0)
    def _():
        g_ref[...] = jnp.zeros_like(g_ref)

    row = lax.broadcasted_iota(jnp.int32, (CHUNK, LANES), 0)
    lane = lax.broadcasted_iota(jnp.int32, (CHUNK, LANES), 1)
    s_idx = lane & (HEAD - 1)
    lo = lane < HEAD
    strict = (s_idx < row, s_idx > row)
    incl = (s_idx <= row, s_idx >= row)
    eye2 = (s_idx == row).astype(F32)
    r2 = lax.broadcasted_iota(jnp.int32, (LANES, LANES), 0)
    l2 = lax.broadcasted_iota(jnp.int32, (LANES, LANES), 1)
    bdmask = (r2 >= HEAD) == (l2 >= HEAD)

    def bd(x):
        z = jnp.zeros_like(x)
        return jnp.concatenate([jnp.where(lo, x, z), jnp.where(lo, z, x)], axis=0)

    def nn(a, b_):
        return jnp.dot(a.astype(BF16), b_.astype(BF16), preferred_element_type=F32)

    def nt(a, b_):
        return lax.dot_general(a.astype(BF16), b_.astype(BF16), (((1,), (1,)), ((), ())),
                               preferred_element_type=F32)

    def tn(a, b_):
        return lax.dot_general(a.astype(BF16), b_.astype(BF16), (((0,), (0,)), ((), ())),
                               preferred_element_type=F32)

    refs = ((qkf, qrf, kbf, kdf, etf, vf, yf_ref), (qkb, qrb, kbb, kdb, etb, vb, yb_ref))
    units = [(d, j) for d in range(2) for j in range(pb)]
    n = range(len(units))

    def tile(k, u):
        d, j = units[u]
        return refs[d][k][:, j * LANES:(j + 1) * LANES]

    q2 = [jnp.concatenate([tile(0, u), tile(1, u)], axis=0) for u in n]
    kb = [tile(2, u) for u in n]
    kd = [tile(3, u) for u in n]
    v = [tile(5, u) for u in n]
    g0 = [g_ref[units[u][0], units[u][1]] for u in n]

    ab = [nt(q2[u], bd(kb[u])) for u in n]
    ak = [nt(q2[u], bd(kd[u])) for u in n]
    qh = [nt(q2[u], g0[u]) for u in n]
    a1b = [jnp.where(strict[units[u][0]], ab[u][:CHUNK], 0.0) for u in n]
    a1k = [jnp.where(strict[units[u][0]], ak[u][:CHUNK], 0.0) for u in n]
    a2b = [jnp.where(incl[units[u][0]], ab[u][CHUNK:], 0.0) for u in n]
    a2k = [jnp.where(incl[units[u][0]], ak[u][CHUNK:], 0.0) for u in n]
    vbd = [bd(v[u]) for u in n]
    rhs = [qh[u][:CHUNK] + nn(a1k[u], vbd[u]) for u in n]
    x = [-a1b[u] for u in n]
    t = [eye2 + x[u] for u in n]
    p = [nn(x[u], bd(x[u])) for u in n]
    for _ in range(int(math.log2(CHUNK)) - 2):
        tp = [nn(jnp.concatenate([t[u], p[u]], axis=0), bd(p[u])) for u in n]
        t = [t[u] + tp[u][:CHUNK] for u in n]
        p = [tp[u][CHUNK:] for u in n]
    t = [t[u] + nn(t[u], bd(p[u])) for u in n]
    uu = [nn(t[u], bd(rhs[u])) for u in n]
    for u in n:
        d, j = units[u]
        lhs = jnp.concatenate([a2k[u].astype(BF16), a2b[u].astype(BF16)], axis=1)
        rhs2 = jnp.concatenate([vbd[u], bd((-uu[u]).astype(BF16))], axis=0)
        refs[d][6][:, j * LANES:(j + 1) * LANES] = qh[u][CHUNK:] + nn(lhs, rhs2)
    for u in n:
        d, j = units[u]
        vu = jnp.concatenate([v[u], (-uu[u]).astype(BF16)], axis=0)
        kk = jnp.concatenate([kd[u], kb[u]], axis=0)
        ginc = jnp.where(bdmask, tn(vu, kk), 0.0)
        g_ref[d, j] = tile(4, u) * (g0[u] + ginc)


def _rwkv_scan(qk, qr, kb, kd, et, v, pb=8):
    _, b, t, _ = qk.shape
    nc = t // CHUNK
    w = pb * LANES

    def tspec(d, lead):
        def imap(bi, gi, ci):
            cidx = ci if d == 0 else nc - 1 - ci
            return ((d,) if lead else ()) + (bi, cidx, gi)
        return pl.BlockSpec(((None,) if lead else ()) + (None, CHUNK, w), imap)

    def espec(d):
        def imap(bi, gi, ci):
            return (d, bi, ci if d == 0 else nc - 1 - ci, 0, gi)
        return pl.BlockSpec((None, None, None, 1, w), imap)

    in_specs, args = [], []
    for d in range(2):
        in_specs += [tspec(d, True)] * 4 + [espec(d), tspec(d, False)]
        args += [qk, qr, kb, kd, et, v]
    yshape = jax.ShapeDtypeStruct((b, t, D_RWKV), F32)
    return pl.pallas_call(
        functools.partial(_scan_kernel, pb=pb),
        grid=(b, D_RWKV // w, nc),
        in_specs=in_specs,
        out_specs=[tspec(0, False), tspec(1, False)],
        out_shape=[yshape, yshape],
        scratch_shapes=[pltpu.VMEM((2, pb, LANES, LANES), F32)],
        compiler_params=_params(("parallel", "parallel", "arbitrary")),
        name="rwkv_scan",
    )(*args)


def _post_kernel(yf_ref, yb_ref, bonus_ref, g_ref, lg_ref, lb_ref, o_ref):
    li = lax.broadcasted_iota(jnp.int32, (LANES, LANES), 0) >> 6
    lj = lax.broadcasted_iota(jnp.int32, (LANES, LANES), 1) >> 6
    gmat = (li == lj).astype(BF16)
    for g in range(D_RWKV // LANES):
        sl = slice(g * LANES, (g + 1) * LANES)
        y = yf_ref[:, sl] + yb_ref[:, sl]
        mu = jnp.dot(y.astype(BF16), gmat, preferred_element_type=F32) * (1.0 / HEAD)
        dy = y - mu
        var = jnp.dot((dy * dy).astype(BF16), gmat, preferred_element_type=F32) * (1.0 / HEAD)
        yn = dy * lax.rsqrt(var + LNX_EPS)
        out = yn * lg_ref[:, sl] + lb_ref[:, sl] + bonus_ref[:, sl].astype(F32)
        gate = g_ref[:, sl].astype(F32)
        o_ref[:, sl] = (out * (gate * _sigmoid(gate))).astype(o_ref.dtype)


def _rwkv_post(yf, yb, bonus, proj, lnx_g, lnx_b, tt=256):
    b, t, _ = bonus.shape
    tt = min(tt, t)
    tok = pl.BlockSpec((None, tt, D_RWKV), lambda bi, ti: (bi, ti, 0))
    return pl.pallas_call(
        _post_kernel,
        grid=(b, t // tt),
        in_specs=[tok, tok, tok,
                  pl.BlockSpec((None, tt, D_RWKV), lambda bi, ti: (bi, ti, COL_GR // D_RWKV)),
                  pl.BlockSpec((1, D_RWKV), lambda bi, ti: (0, 0)),
                  pl.BlockSpec((1, D_RWKV), lambda bi, ti: (0, 0))],
        out_specs=tok,
        out_shape=jax.ShapeDtypeStruct((b, t, D_RWKV), BF16),
        compiler_params=_params(("parallel", "parallel")),
        name="rwkv_post",
    )(yf, yb, bonus, proj, lnx_g, lnx_b)


def _pad_rows(w, rows):
    pad = [(0, 0)] * w.ndim
    pad[-2] = (0, rows - w.shape[-2])
    return jnp.pad(w, pad)


def kernel(x, p, w_in_first, w_in_rest, norm_g, conv_w, w0, w2, a0, a2, k_k, k_a, r_k, lnx_g, lnx_b,
           v0, v2, attn_sink, w_out, ple_norm_g, w_ple, w_pg, final_g):
    b, t, d = x.shape
    m = b * t
    depth = norm_g.shape[0]
    h = x.reshape(m, d)
    v_first = None
    for i in range(depth):
        w_in = w_in_first if i == 0 else w_in_rest[i - 1]
        w_main = jnp.concatenate([w_in[:, :SRC_LW].astype(BF16), w_in[:, SRC_Q:SRC_LV].astype(BF16)], axis=1)
        lora_cols = [w_in[:, SRC_LW:SRC_LA], w_in[:, SRC_LA:SRC_Q]]
        if i > 0:
            lora_cols.append(w_in[:, SRC_LV:SRC_LV + MV_LORA])
        else:
            lora_cols.append(jnp.zeros((d, MV_LORA), w_in.dtype))
        w_lora = jnp.concatenate(
            [jnp.pad(c, ((0, 0), (0, LORA_PAD - c.shape[1]))) for c in lora_cols], axis=1).astype(BF16)

        proj, xn = _norm_matmul(h, norm_g[i], w_main)
        proj = proj.reshape(b, t, N_MAIN)
        codes = _matmul(xn, w_lora, F32, tm=512, tn=3 * LORA_PAD).reshape(b, t, 3 * LORA_PAD)

        vmix = None
        if i > 0:
            vmix = (v_first, v0[i - 1].reshape(1, D_RWKV), _pad_rows(v2[i - 1], LORA_PAD).astype(BF16))
        outs = _rwkv_prep(proj, codes, conv_w[i], w0[i], _pad_rows(w2[i], LORA_PAD).astype(BF16),
                          a0[i], _pad_rows(a2[i], LORA_PAD).astype(BF16),
                          k_k[i].reshape(1, D_RWKV), k_a[i].reshape(1, D_RWKV), r_k[i].reshape(1, D_RWKV), vmix)
        qk, qr, kb, kd, et, vh, bonus = outs
        if i == 0:
            v_first = vh
        yf, yb = _rwkv_scan(qk, qr, kb, kd, et, vh)
        mixed_r = _rwkv_post(yf, yb, bonus, proj, lnx_g[i].reshape(1, D_RWKV), lnx_b[i].reshape(1, D_RWKV))
        mixed_a = _attention(proj, attn_sink[i])

        h = _out_proj(h, mixed_r.reshape(m, D_RWKV), mixed_a.reshape(m, D_ATTN), w_out[i].astype(BF16))
        h = _ple_gate(h, ple_norm_g[i], p[i].reshape(m, PLE_DIM), w_pg[i].astype(BF16), w_ple[i].astype(BF16))
    return _rmsnorm(h, final_g, F32).reshape(b, t, d)
```

```python
import functools
import math

import jax
import jax.numpy as jnp
from jax import lax
from jax.experimental import pallas as pl
from jax.experimental.pallas import tpu as pltpu

F32 = jnp.float32
BF16 = jnp.bfloat16

D_MODEL = 4096
PLE_DIM = 256
D_RWKV = 2048
HEAD = 64
LANES = 128
D_ATTN = 2048
ATTN_HEAD = 128
KV_HEADS = 4
GQA = 4
WINDOW = 128
MV_LORA = 64
LORA_PAD = 128
NORM_EPS = 1e-6
LNX_EPS = 64e-5
LOG2E = 1.4426950408889634
CHUNK = 64

COL_R, COL_K, COL_V, COL_GR = 0, 2048, 4096, 6144
COL_Q, COL_KA, COL_VA, COL_GA = 8192, 10240, 10752, 11264
N_MAIN = 13312
SRC_LW, SRC_LA, SRC_Q, SRC_LV = 8192, 8288, 8384, 13504

VMEM_LIMIT = 56 * 1024 * 1024


def _sigmoid(x):
    return 1.0 / (1.0 + jnp.exp(-x))


def _params(sem, vmem=VMEM_LIMIT):
    return pltpu.CompilerParams(dimension_semantics=sem, vmem_limit_bytes=vmem)


def _norm_rows(x, g):
    ms = jnp.mean(x * x, axis=-1, keepdims=True)
    return x * lax.rsqrt(ms + NORM_EPS) * g


def _rmsnorm_kernel(x_ref, g_ref, o_ref):
    o_ref[...] = _norm_rows(x_ref[...], g_ref[...]).astype(o_ref.dtype)


def _rmsnorm(x, g, out_dtype, tm=256):
    m, d = x.shape
    tm = min(tm, m)
    return pl.pallas_call(
        _rmsnorm_kernel,
        grid=(m // tm,),
        in_specs=[pl.BlockSpec((tm, d), lambda i: (i, 0)),
                  pl.BlockSpec((1, d), lambda i: (0, 0))],
        out_specs=pl.BlockSpec((tm, d), lambda i: (i, 0)),
        out_shape=jax.ShapeDtypeStruct((m, d), out_dtype),
        compiler_params=_params(("parallel",)),
        name="rmsnorm",
    )(x, g.reshape(1, d).astype(F32))


def _mm_kernel(a_ref, w_ref, o_ref):
    o_ref[...] = jnp.dot(a_ref[...], w_ref[...], preferred_element_type=F32).astype(o_ref.dtype)


def _matmul(a, w, out_dtype, tm, tn):
    m, k = a.shape
    n = w.shape[1]
    tm, tn = min(tm, m), min(tn, n)
    return pl.pallas_call(
        _mm_kernel,
        grid=(n // tn, m // tm),
        in_specs=[pl.BlockSpec((tm, k), lambda j, i: (i, 0)),
                  pl.BlockSpec((k, tn), lambda j, i: (0, j))],
        out_specs=pl.BlockSpec((tm, tn), lambda j, i: (i, j)),
        out_shape=jax.ShapeDtypeStruct((m, n), out_dtype),
        compiler_params=_params(("parallel", "parallel")),
        name="in_proj",
    )(a, w)


def _mm_out_kernel(h_ref, a1_ref, a2_ref, w1_ref, w2_ref, o_ref):
    acc = jnp.dot(a1_ref[...], w1_ref[...], preferred_element_type=F32)
    acc = acc + jnp.dot(a2_ref[...], w2_ref[...], preferred_element_type=F32)
    o_ref[...] = h_ref[...] + acc


def _out_proj(h, a1, a2, w, tm=512, tn=1024):
    m, n = h.shape
    k1, k2 = a1.shape[1], a2.shape[1]
    assert k1 == k2
    tm, tn = min(tm, m), min(tn, n)
    return pl.pallas_call(
        _mm_out_kernel,
        grid=(n // tn, m // tm),
        in_specs=[pl.BlockSpec((tm, tn), lambda j, i: (i, j)),
                  pl.BlockSpec((tm, k1), lambda j, i: (i, 0)),
                  pl.BlockSpec((tm, k2), lambda j, i: (i, 0)),
                  pl.BlockSpec((k1, tn), lambda j, i: (0, j)),
                  pl.BlockSpec((k2, tn), lambda j, i: (1, j))],
        out_specs=pl.BlockSpec((tm, tn), lambda j, i: (i, j)),
        out_shape=jax.ShapeDtypeStruct((m, n), F32),
        compiler_params=_params(("parallel", "parallel")),
        name="out_proj",
    )(h, a1, a2, w, w)


def _mm_gate_kernel(h_ref, hn_ref, p_ref, wpg_ref, wple_ref, o_ref):
    gate = _sigmoid(jnp.dot(hn_ref[...], wpg_ref[...], preferred_element_type=F32))
    ple = jnp.dot(p_ref[...].astype(BF16), wple_ref[...], preferred_element_type=F32)
    o_ref[...] = h_ref[...] + ple * gate


def _ple_gate(h, hn, p, wpg, wple, tm=512, tn=1024):
    m, n = h.shape
    k = hn.shape[1]
    kp = p.shape[1]
    tm, tn = min(tm, m), min(tn, n)
    return pl.pallas_call(
        _mm_gate_kernel,
        grid=(n // tn, m // tm),
        in_specs=[pl.BlockSpec((tm, tn), lambda j, i: (i, j)),
                  pl.BlockSpec((tm, k), lambda j, i: (i, 0)),
                  pl.BlockSpec((tm, kp), lambda j, i: (i, 0)),
                  pl.BlockSpec((k, tn), lambda j, i: (0, j)),
                  pl.BlockSpec((kp, tn), lambda j, i: (0, j))],
        out_specs=pl.BlockSpec((tm, tn), lambda j, i: (i, j)),
        out_shape=jax.ShapeDtypeStruct((m, n), F32),
        compiler_params=_params(("parallel", "parallel")),
        name="ple_gate",
    )(h, hn, p, wpg, wple)


def _attn_kernel(sink_ref, q_ref, kp_ref, kc_ref, kn_ref, vp_ref, vc_ref, vn_ref, g_ref, o_ref, bias_ref):
    kvh = pl.program_id(1)
    n = pl.program_id(2)
    nb = pl.num_programs(2)
    blk = WINDOW
    rows, cols = GQA * blk, 3 * blk

    @pl.when(n == 0)
    def _():
        row = lax.broadcasted_iota(jnp.int32, (rows, cols), 0)
        col = lax.broadcasted_iota(jnp.int32, (rows, cols), 1)
        dist = jnp.abs((row & (blk - 1)) - col + blk)
        rgrp = lax.broadcasted_iota(jnp.int32, (rows, 1), 0) >> 7
        head = (kvh * GQA + rgrp + 1).astype(F32)
        slope = jnp.exp2(-8.0 * head / (KV_HEADS * GQA))
        bias_ref[...] = jnp.where(dist <= WINDOW, -slope * dist.astype(F32), -1e30)

    q = q_ref[...]
    qs = jnp.concatenate([q[:, g * blk:(g + 1) * blk] for g in range(GQA)], axis=0)
    kw = jnp.concatenate([kp_ref[...], kc_ref[...], kn_ref[...]], axis=0)
    vw = jnp.concatenate([vp_ref[...], vc_ref[...], vn_ref[...]], axis=0)
    s = lax.dot_general(qs, kw, (((1,), (1,)), ((), ())), preferred_element_type=F32)
    colr = lax.broadcasted_iota(jnp.int32, (1, cols), 1)
    lo = jnp.where(n > 0, 0, blk)
    hi = jnp.where(n < nb - 1, cols, 2 * blk)
    inside = (colr >= lo) & (colr < hi)
    probs, dens = [], []
    for g in range(GQA):
        rs = slice(g * blk, (g + 1) * blk)
        logits = s[rs] * (1.0 / math.sqrt(ATTN_HEAD)) + bias_ref[rs, :]
        logits = jnp.where(inside, logits, -1e30)
        sink = sink_ref[kvh * GQA + g]
        m = jnp.maximum(jnp.max(logits, axis=-1, keepdims=True), sink)
        pr = jnp.exp(logits - m)
        dens.append(jnp.sum(pr, axis=-1, keepdims=True) + jnp.exp(sink - m))
        probs.append(pr.astype(BF16))
    o = jnp.dot(jnp.concatenate(probs, axis=0), vw, preferred_element_type=F32)
    o = jnp.concatenate([o[g * blk:(g + 1) * blk] / dens[g] for g in range(GQA)], axis=1)
    gate = g_ref[...].astype(F32)
    o_ref[...] = (o * (gate * _sigmoid(gate))).astype(o_ref.dtype)


def _attention(proj, sink):
    b, t, _ = proj.shape
    nb = t // WINDOW
    gw = GQA * ATTN_HEAD

    def kv_spec(col0, off):
        def imap(bi, hi, ni):
            return (bi, jnp.clip(ni + off, 0, nb - 1), col0 // ATTN_HEAD + hi)
        return pl.BlockSpec((None, WINDOW, ATTN_HEAD), imap)

    return pl.pallas_call(
        _attn_kernel,
        grid=(b, KV_HEADS, nb),
        in_specs=[pl.BlockSpec(memory_space=pltpu.SMEM),
                  pl.BlockSpec((None, WINDOW, gw), lambda bi, hi, ni: (bi, ni, COL_Q // gw + hi)),
                  kv_spec(COL_KA, -1), kv_spec(COL_KA, 0), kv_spec(COL_KA, 1),
                  kv_spec(COL_VA, -1), kv_spec(COL_VA, 0), kv_spec(COL_VA, 1),
                  pl.BlockSpec((None, WINDOW, gw), lambda bi, hi, ni: (bi, ni, COL_GA // gw + hi))],
        out_specs=pl.BlockSpec((None, WINDOW, gw), lambda bi, hi, ni: (bi, ni, hi)),
        out_shape=jax.ShapeDtypeStruct((b, t, D_ATTN), BF16),
        scratch_shapes=[pltpu.VMEM((GQA * WINDOW, 3 * WINDOW), F32)],
        compiler_params=_params(("parallel", "parallel", "arbitrary")),
        name="window_attn",
    )(sink.astype(F32), proj, proj, proj, proj, proj, proj, proj, proj)


def _prep_kernel(*refs, tt, cw, vmix):
    (r_m, r_p, r_n, k_m, k_p, k_n, v_m, v_p, v_n, code_ref, cwr_ref, cwk_ref, cwv_ref,
     w0_ref, w2_ref, a0_ref, a2_ref, kk_ref, ka_ref, rk_ref) = refs[:20]
    refs = refs[20:]
    if vmix:
        vf_ref, v0_ref, v2_ref = refs[:3]
        refs = refs[3:]
    qk_ref, qr_ref, kb_ref, kd_ref, et_ref, vout_ref, bonus_ref = refs

    tb = pl.program_id(1)
    ntb = pl.num_programs(1)
    rows = lax.broadcasted_iota(jnp.int32, (tt, cw), 0)

    def conv(m_ref, p_ref, n_ref, w_ref):
        x = m_ref[...].astype(F32)
        prev = jnp.where(tb > 0, p_ref[...].astype(F32)[15:16, :], 0.0)
        nxt = jnp.where(tb < ntb - 1, n_ref[...].astype(F32)[0:1, :], 0.0)
        xm = jnp.where(rows == 0, prev, pltpu.roll(x, 1, 0))
        xp = jnp.where(rows == tt - 1, nxt, pltpu.roll(x, tt - 1, 0))
        w = w_ref[...]
        return w[0:1] * xm + w[1:2] * x + w[2:3] * xp

    xr = conv(r_m, r_p, r_n, cwr_ref)
    xk = conv(k_m, k_p, k_n, cwk_ref)
    xv = conv(v_m, v_p, v_n, cwv_ref)

    code = code_ref[...]
    lw = jnp.tanh(code[:, 0:LORA_PAD]).astype(BF16)
    la = code[:, LORA_PAD:2 * LORA_PAD].astype(BF16)

    if vmix:
        lv = code[:, 2 * LORA_PAD:3 * LORA_PAD].astype(BF16)
        mixg = _sigmoid(v0_ref[...] + jnp.dot(lv, v2_ref[...], preferred_element_type=F32))
        v = xv + (vf_ref[...].astype(F32) - xv) * mixg
    else:
        v = xv
    vout_ref[...] = v.astype(vout_ref.dtype)

    li = lax.broadcasted_iota(jnp.int32, (LANES, LANES), 0) >> 6
    lj = lax.broadcasted_iota(jnp.int32, (LANES, LANES), 1) >> 6
    gmat = (li == lj).astype(BF16)

    def head_sum(x):
        parts = [jnp.dot(x[:, g * LANES:(g + 1) * LANES].astype(BF16), gmat, preferred_element_type=F32)
                 for g in range(cw // LANES)]
        return jnp.concatenate(parts, axis=1)

    kkn = xk * kk_ref[...]
    kappa = kkn / jnp.maximum(jnp.sqrt(head_sum(kkn * kkn)), 1e-12)
    ka = ka_ref[...]

    ti = lax.broadcasted_iota(jnp.int32, (tt, tt), 0)
    tj = lax.broadcasted_iota(jnp.int32, (tt, tt), 1)
    same = (ti >> 6) == (tj >> 6)
    tri = (jnp.where(same & (tj <= ti), 1.0, 0.0).astype(BF16),
           jnp.where(same & (tj >= ti), 1.0, 0.0).astype(BF16))
    ones = jnp.where(same, 1.0, 0.0).astype(BF16)

    a_sum = None
    for d in range(2):
        wl = w0_ref[d:d + 1, :] + jnp.dot(lw, w2_ref[d], preferred_element_type=F32)
        z = -wl
        softplus = jnp.maximum(z, 0.0) + jnp.log(1.0 + jnp.exp(-jnp.abs(z)))
        ld = jnp.exp(-softplus - 0.5) * (-LOG2E)
        a = _sigmoid(a0_ref[d:d + 1, :] + jnp.dot(la, a2_ref[d], preferred_element_type=F32))
        a_sum = a if a_sum is None else a_sum + a
        ld_hi = ld.astype(BF16)
        ld_lo = (ld - ld_hi.astype(F32)).astype(BF16)
        sel = jnp.concatenate([tri[d], ones], axis=0)
        cs2 = (jnp.dot(sel, ld_hi, preferred_element_type=F32)
               + jnp.dot(sel, ld_lo, preferred_element_type=F32))
        cs, tot = cs2[:tt], cs2[tt:]
        e_neg = jnp.exp2(-cs)
        qk_ref[d] = (kappa * jnp.exp2(cs - ld)).astype(BF16)
        qr_ref[d] = (xr * jnp.exp2(cs)).astype(BF16)
        kb_ref[d] = (kappa * a * e_neg).astype(BF16)
        kd_ref[d] = (xk * (1.0 + (a - 1.0) * ka) * e_neg).astype(BF16)
        for cc in range(tt // CHUNK):
            et_ref[d, cc] = jnp.exp2(tot[cc * CHUNK:cc * CHUNK + 1, :])

    k_bonus = xk * (1.0 + (a_sum * 0.5 - 1.0) * ka)
    bonus = head_sum(xr * k_bonus * rk_ref[...]) * v
    bonus_ref[...] = bonus.astype(bonus_ref.dtype)


def _rwkv_prep(proj, codes, conv_w, w0, w2, a0, a2, k_k, k_a, r_k, vmix, tt=256, cw=512):
    b, t, _ = proj.shape
    tt = min(tt, t)
    nc = t // CHUNK
    ntb = t // tt
    hb = 16

    def main_spec(col0):
        return pl.BlockSpec((None, tt, cw), lambda bi, ti, ci: (bi, ti, col0 // cw + ci))

    def prev_spec(col0):
        return pl.BlockSpec((None, hb, cw),
                            lambda bi, ti, ci: (bi, jnp.maximum(ti * (tt // hb) - 1, 0), col0 // cw + ci))

    def next_spec(col0):
        return pl.BlockSpec((None, hb, cw),
                            lambda bi, ti, ci: (bi, jnp.minimum((ti + 1) * (tt // hb), t // hb - 1), col0 // cw + ci))

    def chan_spec(rows_, col0=0):
        return pl.BlockSpec((rows_, cw), lambda bi, ti, ci: (0, col0 // cw + ci))

    lora_spec = pl.BlockSpec((2, LORA_PAD, cw), lambda bi, ti, ci: (0, 0, ci))
    in_specs = []
    args = []
    for col0 in (COL_R, COL_K, COL_V):
        in_specs += [main_spec(col0), prev_spec(col0), next_spec(col0)]
        args += [proj, proj, proj]
    in_specs += [pl.BlockSpec((None, tt, 3 * LORA_PAD), lambda bi, ti, ci: (bi, ti, 0)),
                 chan_spec(3, COL_R), chan_spec(3, COL_K), chan_spec(3, COL_V),
                 chan_spec(2), lora_spec, chan_spec(2), lora_spec,
                 chan_spec(1), chan_spec(1), chan_spec(1)]
    args += [codes, conv_w, conv_w, conv_w, w0, w2, a0, a2, k_k, k_a, r_k]
    if vmix is not None:
        v_first, v0, v2 = vmix
        in_specs += [pl.BlockSpec((None, tt, cw), lambda bi, ti, ci: (bi, ti, ci)),
                     chan_spec(1),
                     pl.BlockSpec((LORA_PAD, cw), lambda bi, ti, ci: (0, ci))]
        args += [v_first, v0, v2]

    dir_spec = pl.BlockSpec((2, None, tt, cw), lambda bi, ti, ci: (0, bi, ti, ci))
    tok_spec = pl.BlockSpec((None, tt, cw), lambda bi, ti, ci: (bi, ti, ci))
    et_spec = pl.BlockSpec((2, None, tt // CHUNK, 1, cw), lambda bi, ti, ci: (0, bi, ti, 0, ci))
    dir_shape = jax.ShapeDtypeStruct((2, b, t, D_RWKV), BF16)
    tok_shape = jax.ShapeDtypeStruct((b, t, D_RWKV), BF16)
    out_specs = [dir_spec, dir_spec, dir_spec, dir_spec, et_spec, tok_spec, tok_spec]
    out_shape = [dir_shape, dir_shape, dir_shape, dir_shape,
                 jax.ShapeDtypeStruct((2, b, nc, 1, D_RWKV), F32), tok_shape, tok_shape]
    return pl.pallas_call(
        functools.partial(_prep_kernel, tt=tt, cw=cw, vmix=vmix is not None),
        grid=(b, ntb, D_RWKV // cw),
        in_specs=in_specs,
        out_specs=out_specs,
        out_shape=out_shape,
        compiler_params=_params(("parallel", "parallel", "parallel")),
        name="rwkv_prep",
    )(*args)


def _scan_kernel(qkf, qrf, kbf, kdf, etf, vf, qkb, qrb, kbb, kdb, etb, vb, yf_ref, yb_ref, g_ref, *, pb):
    c = pl.program_id(2)

    @pl.when(c == 0)
    def _():
        g_ref[...] = jnp.zeros_like(g_ref)

    row = lax.broadcasted_iota(jnp.int32, (CHUNK, LANES), 0)
    lane = lax.broadcasted_iota(jnp.int32, (CHUNK, LANES), 1)
    s_idx = lane & (HEAD - 1)
    lo = lane < HEAD
    strict = (s_idx < row, s_idx > row)
    incl = (s_idx <= row, s_idx >= row)
    eye2 = (s_idx == row).astype(F32)
    r2 = lax.broadcasted_iota(jnp.int32, (LANES, LANES), 0)
    l2 = lax.broadcasted_iota(jnp.int32, (LANES, LANES), 1)
    bdmask = (r2 >= HEAD) == (l2 >= HEAD)

    def bd(x):
        z = jnp.zeros_like(x)
        return jnp.concatenate([jnp.where(lo, x, z), jnp.where(lo, z, x)], axis=0)

    def nn(a, b_):
        return jnp.dot(a.astype(BF16), b_.astype(BF16), preferred_element_type=F32)

    def nt(a, b_):
        return lax.dot_general(a.astype(BF16), b_.astype(BF16), (((1,), (1,)), ((), ())),
                               preferred_element_type=F32)

    def tn(a, b_):
        return lax.dot_general(a.astype(BF16), b_.astype(BF16), (((0,), (0,)), ((), ())),
                               preferred_element_type=F32)

    refs = ((qkf, qrf, kbf, kdf, etf, vf, yf_ref), (qkb, qrb, kbb, kdb, etb, vb, yb_ref))
    units = [(d, j) for d in range(2) for j in range(pb)]
    n = range(len(units))

    def tile(k, u):
        d, j = units[u]
        return refs[d][k][:, j * LANES:(j + 1) * LANES]

    q2 = [jnp.concatenate([tile(0, u), tile(1, u)], axis=0) for u in n]
    kb = [tile(2, u) for u in n]
    kd = [tile(3, u) for u in n]
    v = [tile(5, u) for u in n]
    g0 = [g_ref[units[u][0], units[u][1]] for u in n]

    ab = [nt(q2[u], bd(kb[u])) for u in n]
    ak = [nt(q2[u], bd(kd[u])) for u in n]
    qh = [nt(q2[u], g0[u]) for u in n]
    a1b = [jnp.where(strict[units[u][0]], ab[u][:CHUNK], 0.0) for u in n]
    a1k = [jnp.where(strict[units[u][0]], ak[u][:CHUNK], 0.0) for u in n]
    a2b = [jnp.where(incl[units[u][0]], ab[u][CHUNK:], 0.0) for u in n]
    a2k = [jnp.where(incl[units[u][0]], ak[u][CHUNK:], 0.0) for u in n]
    vbd = [bd(v[u]) for u in n]
    rhs = [qh[u][:CHUNK] + nn(a1k[u], vbd[u]) for u in n]
    x = [-a1b[u] for u in n]
    t = [eye2 + x[u] for u in n]
    p = [nn(x[u], bd(x[u])) for u in n]
    for _ in range(int(math.log2(CHUNK)) - 2):
        tp = [nn(jnp.concatenate([t[u], p[u]], axis=0), bd(p[u])) for u in n]
        t = [t[u] + tp[u][:CHUNK] for u in n]
        p = [tp[u][CHUNK:] for u in n]
    t = [t[u] + nn(t[u], bd(p[u])) for u in n]
    uu = [nn(t[u], bd(rhs[u])) for u in n]
    for u in n:
        d, j = units[u]
        lhs = jnp.concatenate([a2k[u].astype(BF16), a2b[u].astype(BF16)], axis=1)
        rhs2 = jnp.concatenate([vbd[u], bd((-uu[u]).astype(BF16))], axis=0)
        refs[d][6][:, j * LANES:(j + 1) * LANES] = qh[u][CHUNK:] + nn(lhs, rhs2)
    for u in n:
        d, j = units[u]
        vu = jnp.concatenate([v[u], (-uu[u]).astype(BF16)], axis=0)
        kk = jnp.concatenate([kd[u], kb[u]], axis=0)
        ginc = jnp.where(bdmask, tn(vu, kk), 0.0)
        g_ref[d, j] = tile(4, u) * (g0[u] + ginc)


def _rwkv_scan(qk, qr, kb, kd, et, v, pb=16):
    _, b, t, _ = qk.shape
    nc = t // CHUNK
    w = pb * LANES

    def tspec(d, lead):
        def imap(bi, gi, ci):
            cidx = ci if d == 0 else nc - 1 - ci
            return ((d,) if lead else ()) + (bi, cidx, gi)
        return pl.BlockSpec(((None,) if lead else ()) + (None, CHUNK, w), imap)

    def espec(d):
        def imap(bi, gi, ci):
            return (d, bi, ci if d == 0 else nc - 1 - ci, 0, gi)
        return pl.BlockSpec((None, None, None, 1, w), imap)

    in_specs, args = [], []
    for d in range(2):
        in_specs += [tspec(d, True)] * 4 + [espec(d), tspec(d, False)]
        args += [qk, qr, kb, kd, et, v]
    yshape = jax.ShapeDtypeStruct((b, t, D_RWKV), F32)
    return pl.pallas_call(
        functools.partial(_scan_kernel, pb=pb),
        grid=(b, D_RWKV // w, nc),
        in_specs=in_specs,
        out_specs=[tspec(0, False), tspec(1, False)],
        out_shape=[yshape, yshape],
        scratch_shapes=[pltpu.VMEM((2, pb, LANES, LANES), F32)],
        compiler_params=_params(("parallel", "parallel", "arbitrary")),
        name="rwkv_scan",
    )(*args)


def _post_kernel(yf_ref, yb_ref, bonus_ref, g_ref, lg_ref, lb_ref, o_ref):
    li = lax.broadcasted_iota(jnp.int32, (LANES, LANES), 0) >> 6
    lj = lax.broadcasted_iota(jnp.int32, (LANES, LANES), 1) >> 6
    gmat = (li == lj).astype(BF16)
    for g in range(D_RWKV // LANES):
        sl = slice(g * LANES, (g + 1) * LANES)
        y = yf_ref[:, sl] + yb_ref[:, sl]
        mu = jnp.dot(y.astype(BF16), gmat, preferred_element_type=F32) * (1.0 / HEAD)
        dy = y - mu
        var = jnp.dot((dy * dy).astype(BF16), gmat, preferred_element_type=F32) * (1.0 / HEAD)
        yn = dy * lax.rsqrt(var + LNX_EPS)
        out = yn * lg_ref[:, sl] + lb_ref[:, sl] + bonus_ref[:, sl].astype(F32)
        gate = g_ref[:, sl].astype(F32)
        o_ref[:, sl] = (out * (gate * _sigmoid(gate))).astype(o_ref.dtype)


def _rwkv_post(yf, yb, bonus, proj, lnx_g, lnx_b, tt=256):
    b, t, _ = bonus.shape
    tt = min(tt, t)
    tok = pl.BlockSpec((None, tt, D_RWKV), lambda bi, ti: (bi, ti, 0))
    return pl.pallas_call(
        _post_kernel,
        grid=(b, t // tt),
        in_specs=[tok, tok, tok,
                  pl.BlockSpec((None, tt, D_RWKV), lambda bi, ti: (bi, ti, COL_GR // D_RWKV)),
                  pl.BlockSpec((1, D_RWKV), lambda bi, ti: (0, 0)),
                  pl.BlockSpec((1, D_RWKV), lambda bi, ti: (0, 0))],
        out_specs=tok,
        out_shape=jax.ShapeDtypeStruct((b, t, D_RWKV), BF16),
        compiler_params=_params(("parallel", "parallel")),
        name="rwkv_post",
    )(yf, yb, bonus, proj, lnx_g, lnx_b)


def _pad_rows(w, rows):
    pad = [(0, 0)] * w.ndim
    pad[-2] = (0, rows - w.shape[-2])
    return jnp.pad(w, pad)


def kernel(x, p, w_in_first, w_in_rest, norm_g, conv_w, w0, w2, a0, a2, k_k, k_a, r_k, lnx_g, lnx_b,
           v0, v2, attn_sink, w_out, ple_norm_g, w_ple, w_pg, final_g):
    b, t, d = x.shape
    m = b * t
    depth = norm_g.shape[0]
    h = x.reshape(m, d)
    v_first = None
    for i in range(depth):
        w_in = w_in_first if i == 0 else w_in_rest[i - 1]
        w_main = jnp.concatenate([w_in[:, :SRC_LW].astype(BF16), w_in[:, SRC_Q:SRC_LV].astype(BF16)], axis=1)
        lora_cols = [w_in[:, SRC_LW:SRC_LA], w_in[:, SRC_LA:SRC_Q]]
        if i > 0:
            lora_cols.append(w_in[:, SRC_LV:SRC_LV + MV_LORA])
        else:
            lora_cols.append(jnp.zeros((d, MV_LORA), w_in.dtype))
        w_lora = jnp.concatenate(
            [jnp.pad(c, ((0, 0), (0, LORA_PAD - c.shape[1]))) for c in lora_cols], axis=1).astype(BF16)

        xn = _rmsnorm(h, norm_g[i], BF16)
        proj = _matmul(xn, w_main, BF16, tm=512, tn=1024).reshape(b, t, N_MAIN)
        codes = _matmul(xn, w_lora, F32, tm=512, tn=3 * LORA_PAD).reshape(b, t, 3 * LORA_PAD)

        vmix = None
        if i > 0:
            vmix = (v_first, v0[i - 1].reshape(1, D_RWKV), _pad_rows(v2[i - 1], LORA_PAD).astype(BF16))
        outs = _rwkv_prep(proj, codes, conv_w[i], w0[i], _pad_rows(w2[i], LORA_PAD).astype(BF16),
                          a0[i], _pad_rows(a2[i], LORA_PAD).astype(BF16),
                          k_k[i].reshape(1, D_RWKV), k_a[i].reshape(1, D_RWKV), r_k[i].reshape(1, D_RWKV), vmix)
        qk, qr, kb, kd, et, vh, bonus = outs
        if i == 0:
            v_first = vh
        yf, yb = _rwkv_scan(qk, qr, kb, kd, et, vh)
        mixed_r = _rwkv_post(yf, yb, bonus, proj, lnx_g[i].reshape(1, D_RWKV), lnx_b[i].reshape(1, D_RWKV))
        mixed_a = _attention(proj, attn_sink[i])

        h = _out_proj(h, mixed_r.reshape(m, D_RWKV), mixed_a.reshape(m, D_ATTN), w_out[i].astype(BF16))
        hn = _rmsnorm(h, ple_norm_g[i], BF16)
        h = _ple_gate(h, hn, p[i].reshape(m, PLE_DIM), w_pg[i].astype(BF16), w_ple[i].astype(BF16))
    return _rmsnorm(h, final_g, F32).reshape(b, t, d)
```
